```python
import math
import functools
import jax
import jax.numpy as jnp
from jax import lax
import numpy as np

D_MODEL = 2048
BATCH = 4
SEQ = 2048
DEPTH = 4
DEC_BATCH = 8
DEC_SEQ = 1
PAST_LEN = 16384
PAGE_SIZE = 128

D_MIX = D_MODEL
ATT_HEADS = 6
ATT_DH = 64
ATT_QK = 2 * ATT_DH
ATT_DV = 2 * ATT_DH
ATT_WIDTH = ATT_HEADS * ATT_DV
MLSTM_HEADS = 4
MLSTM_DH = 192
MLSTM_WIDTH = MLSTM_HEADS * MLSTM_DH
MLSTM_CHUNK = 64
CONV_DIM = D_MIX - ATT_WIDTH - MLSTM_WIDTH
CONV_W = 3
D_FF = 5632
Q_BLOCK = 128
NORM_EPS = 1e-6
N_IN = 3 * ATT_HEADS * ATT_QK + 4 * MLSTM_WIDTH + 2 * MLSTM_HEADS + 3 * CONV_DIM

kernel_name = 'hybrid_diffattn_mlstm_shortconv_step'


def rmsnorm(x, g):
    xf = x.astype(jnp.float32)
    y = xf * lax.rsqrt(jnp.mean(xf * xf, axis=-1, keepdims=True) + NORM_EPS)
    return (y * g.astype(jnp.float32)).astype(x.dtype)


def alibi_slopes(n_heads):
    def geometric(n):
        start = 2.0 ** (-8.0 / n)
        return [start ** (i + 1) for i in range(n)]
    closest = 2 ** int(math.floor(math.log2(n_heads)))
    slopes = geometric(closest)
    if closest < n_heads:
        slopes = slopes + geometric(2 * closest)[0::2][: n_heads - closest]
    return jnp.asarray(np.asarray(slopes, dtype=np.float32))


def split_in_proj(p):
    sizes = (ATT_HEADS * ATT_QK, ATT_HEADS * ATT_QK, ATT_WIDTH,
             MLSTM_WIDTH, MLSTM_WIDTH, MLSTM_WIDTH, MLSTM_WIDTH, 2 * MLSTM_HEADS,
             CONV_DIM, CONV_DIM, CONV_DIM)
    idx = [int(s) for s in np.cumsum(sizes)[:-1]]
    return jnp.split(p, idx, axis=-1)


def causal_dwconv(u, buf, w):
    T = u.shape[1]
    up = jnp.concatenate([buf.astype(u.dtype), u], axis=1)
    y = up[:, 0:T] * w[0]
    for j in range(1, CONV_W):
        y = y + up[:, j:j + T] * w[j]
    return y, up[:, up.shape[1] - (CONV_W - 1):]


def diff_attend(q, ks, vs, qpos, kposs, slopes, lam):
    qf = q.astype(jnp.float32) * (ATT_DH ** -0.5)
    scores = []
    for k, kpos in zip(ks, kposs):
        s = jnp.einsum('bqhcd,bkhcd->bhcqk', qf, k.astype(jnp.float32))
        dist = (qpos[:, None] - kpos[None, :]).astype(jnp.float32)
        bias = jnp.where(dist >= 0, -slopes[:, None, None] * dist, -jnp.inf)
        scores.append(s + bias[None, :, None])
    p = jax.nn.softmax(jnp.concatenate(scores, axis=-1), axis=-1)
    a = p[:, :, 0] - lam * p[:, :, 1]
    out = None
    off = 0
    for v in vs:
        n = v.shape[1]
        o = jnp.einsum('bhqk,bkhv->bqhv', a[..., off:off + n], v.astype(jnp.float32))
        out = o if out is None else out + o
        off += n
    return out


def attend_prompt(q, k, v, lam, slopes):
    B, S = q.shape[0], q.shape[1]
    qb = math.gcd(Q_BLOCK, S)
    nb = S // qb
    pos = jnp.arange(S, dtype=jnp.int32)
    q_blocks = jnp.moveaxis(q.reshape(B, nb, qb, ATT_HEADS, 2, ATT_DH), 1, 0)
    qpos_blocks = pos.reshape(nb, qb)
    out = lax.map(lambda blk: diff_attend(blk[0], (k,), (v,), blk[1], (pos,), slopes, lam),
                  (q_blocks, qpos_blocks))
    return jnp.moveaxis(out, 0, 1).reshape(B, S, ATT_HEADS, ATT_DV)


def attend_sample(q, k, v, lam, slopes, k_pool, v_pool, page_table):
    Bd, T = q.shape[0], q.shape[1]
    past = page_table.shape[1] * PAGE_SIZE
    kp = k_pool[page_table].reshape(Bd, past, ATT_HEADS, 2, ATT_DH)
    vp = v_pool[page_table].reshape(Bd, past, ATT_HEADS, ATT_DV)
    past_pos = jnp.arange(past, dtype=jnp.int32)
    new_pos = past + jnp.arange(T, dtype=jnp.int32)
    return diff_attend(q, (kp, k), (vp, v), new_pos, (past_pos, new_pos), slopes, lam)


def mlstm_chunkwise(q, k, v, ig, fg, c0, n0, m0):
    B, T, H, Dh = q.shape
    L = math.gcd(MLSTM_CHUNK, T)
    nc = T // L
    f32 = jnp.float32

    def chunks(a):
        return jnp.moveaxis(a.astype(f32).reshape((B, nc, L) + a.shape[2:]), 1, 0)

    qs = chunks(q)
    ks = chunks(k.astype(f32) * (Dh ** -0.5))
    vs = chunks(v)
    igs = chunks(ig)
    lfs = chunks(jax.nn.log_sigmoid(fg.astype(f32)))
    causal = jnp.tril(jnp.ones((L, L), dtype=bool))

    def step(carry, inp):
        c, n, m = carry
        qc, kc, vc, ic, lf = inp
        b = jnp.cumsum(lf, axis=1)
        d = b[:, :, None, :] - b[:, None, :, :] + ic[:, None, :, :]
        d = jnp.where(causal[None, :, :, None], d, -jnp.inf)
        inter = b + m[:, None, :]
        m_t = jnp.maximum(inter, jnp.max(d, axis=2))
        w_intra = jnp.exp(d - m_t[:, :, None, :])
        w_state = jnp.exp(inter - m_t)
        a = w_intra * jnp.einsum('bthd,bshd->btsh', qc, kc)
        num = (jnp.einsum('btsh,bshv->bthv', a, vc)
               + w_state[..., None] * jnp.einsum('bhvk,bthk->bthv', c, qc))
        den = jnp.sum(a, axis=2) + w_state * jnp.einsum('bhk,bthk->bth', n, qc)
        h = num / jnp.maximum(jnp.abs(den), jnp.exp(-m_t))[..., None]
        m_new = m_t[:, -1]
        w_s = jnp.exp(b[:, -1:] - b + ic - m_new[:, None])
        decay = jnp.exp(b[:, -1] + m - m_new)
        c_new = decay[..., None, None] * c + jnp.einsum('bsh,bshv,bshk->bhvk', w_s, vc, kc)
        n_new = decay[..., None] * n + jnp.einsum('bsh,bshk->bhk', w_s, kc)
        return (c_new, n_new, m_new), h

    (c1, n1, m1), hs = lax.scan(step, (c0.astype(f32), n0.astype(f32), m0.astype(f32)),
                                (qs, ks, vs, igs, lfs))
    h = jnp.moveaxis(hs, 0, 1).reshape(B, T, H, Dh)
    return h, c1, n1, m1


def hybrid_layer(x, attn_fn, c0, n0, m0, cbuf, fbuf, layer_idx,
                 g_mix, w_in, b_if, att_lambda, att_subln, mlstm_norm, conv_w, w_out,
                 g_ffn, w_up, ffn_conv_w, w_down):
    B, T, _ = x.shape
    h = rmsnorm(x, g_mix)
    proj = h @ w_in
    aq, ak, av, mq, mk, mv, mo, mg, cb, cc, cx = split_in_proj(proj)

    lam_init = 0.8 - 0.6 * math.exp(-0.3 * layer_idx)
    lam = (jnp.exp(jnp.sum(att_lambda[0].astype(jnp.float32) * att_lambda[1].astype(jnp.float32)))
           - jnp.exp(jnp.sum(att_lambda[2].astype(jnp.float32) * att_lambda[3].astype(jnp.float32)))
           + lam_init)
    q = aq.reshape(B, T, ATT_HEADS, 2, ATT_DH)
    k = ak.reshape(B, T, ATT_HEADS, 2, ATT_DH)
    v = av.reshape(B, T, ATT_HEADS, ATT_DV)
    a = attn_fn(q, k, v, lam)
    a = rmsnorm(a, att_subln) * (1.0 - lam_init)

    gates = (mg + b_if).astype(jnp.float32)
    ig, fg = gates[..., :MLSTM_HEADS], gates[..., MLSTM_HEADS:]
    hm, c1, n1, m1 = mlstm_chunkwise(mq.reshape(B, T, MLSTM_HEADS, MLSTM_DH),
                                     mk.reshape(B, T, MLSTM_HEADS, MLSTM_DH),
                                     mv.reshape(B, T, MLSTM_HEADS, MLSTM_DH),
                                     ig, fg, c0, n0, m0)
    hm = rmsnorm(hm, mlstm_norm.reshape(MLSTM_HEADS, MLSTM_DH))
    hm = jax.nn.sigmoid(mo.reshape(B, T, MLSTM_HEADS, MLSTM_DH).astype(jnp.float32)) * hm

    yc, cbuf1 = causal_dwconv(cc * cx, cbuf, conv_w)
    yc = cb * yc

    mix = jnp.concatenate([a.reshape(B, T, ATT_WIDTH).astype(x.dtype),
                           hm.reshape(B, T, MLSTM_WIDTH).astype(x.dtype),
                           yc.astype(x.dtype)], axis=-1)
    x = x + mix @ w_out

    h2 = rmsnorm(x, g_ffn)
    up, fbuf1 = causal_dwconv(h2 @ w_up, fbuf, ffn_conv_w)
    gate, val = jnp.split(up, 2, axis=-1)
    x = x + (jax.nn.silu(gate) * val) @ w_down
    return x, k.reshape(B, T, ATT_HEADS, ATT_QK), v, c1, n1, m1, cbuf1, fbuf1


def setup_inputs(seed: int = 0) -> dict:
    key = jax.random.key(seed)
    ks = jax.random.split(key, 24)
    n_pages = PAST_LEN // PAGE_SIZE
    n_pool = (DEC_BATCH * n_pages * 5) // 4
    nrm = jax.random.normal
    f32 = jnp.float32
    page_table = jax.random.permutation(ks[9], n_pool)[: DEC_BATCH * n_pages]
    page_table = page_table.reshape(DEC_BATCH, n_pages).astype(jnp.int32)
    b_i = 0.01 * nrm(ks[12], (DEPTH, MLSTM_HEADS), f32)
    b_f = jnp.linspace(3.0, 6.0, MLSTM_HEADS, dtype=f32)[None, :] + 0.01 * nrm(ks[13], (DEPTH, MLSTM_HEADS), f32)
    return {
        'x_prompt': nrm(ks[0], (BATCH, SEQ, D_MODEL), f32),
        'x_sample': nrm(ks[1], (DEC_BATCH, DEC_SEQ, D_MODEL), f32),
        'cache_k': nrm(ks[2], (DEPTH, n_pool, PAGE_SIZE, ATT_HEADS, ATT_QK), f32),
        'cache_v': nrm(ks[3], (DEPTH, n_pool, PAGE_SIZE, ATT_HEADS, ATT_DV), f32),
        'state_mlstm_c': 0.1 * nrm(ks[4], (DEPTH, DEC_BATCH, MLSTM_HEADS, MLSTM_DH, MLSTM_DH), f32),
        'state_mlstm_n': 0.1 * nrm(ks[5], (DEPTH, DEC_BATCH, MLSTM_HEADS, MLSTM_DH), f32),
        'state_mlstm_m': nrm(ks[6], (DEPTH, DEC_BATCH, MLSTM_HEADS), f32),
        'state_conv': nrm(ks[7], (DEPTH, DEC_BATCH, CONV_W - 1, CONV_DIM), f32),
        'state_ffn_conv': nrm(ks[8], (DEPTH, DEC_BATCH, CONV_W - 1, 2 * D_FF), f32),
        'page_table': page_table,
        'g_mix': 1.0 + 0.02 * nrm(ks[10], (DEPTH, D_MODEL), f32),
        'w_in': nrm(ks[11], (DEPTH, D_MODEL, N_IN), f32) * D_MODEL ** -0.5,
        'b_if': jnp.concatenate([b_i, b_f], axis=-1),
        'att_lambda': 0.1 * nrm(ks[14], (DEPTH, 4, ATT_DH), f32),
        'att_subln': 1.0 + 0.02 * nrm(ks[15], (DEPTH, ATT_DV), f32),
        'mlstm_norm': 1.0 + 0.02 * nrm(ks[16], (DEPTH, MLSTM_WIDTH), f32),
        'conv_w': nrm(ks[17], (DEPTH, CONV_W, CONV_DIM), f32) * CONV_W ** -0.5,
        'w_out': nrm(ks[18], (DEPTH, D_MIX, D_MODEL), f32) * D_MIX ** -0.5,
        'g_ffn': 1.0 + 0.02 * nrm(ks[19], (DEPTH, D_MODEL), f32),
        'w_up': nrm(ks[20], (DEPTH, D_MODEL, 2 * D_FF), f32) * D_MODEL ** -0.5,
        'ffn_conv_w': nrm(ks[21], (DEPTH, CONV_W, 2 * D_FF), f32) * CONV_W ** -0.5,
        'w_down': nrm(ks[22], (DEPTH, D_FF, D_MODEL), f32) * D_FF ** -0.5,
        'g_final': 1.0 + 0.02 * nrm(ks[23], (D_MODEL,), f32),
    }


def reference(x_prompt, x_sample, cache_k, cache_v, state_mlstm_c, state_mlstm_n, state_mlstm_m,
              state_conv, state_ffn_conv, page_table,
              g_mix, w_in, b_if, att_lambda, att_subln, mlstm_norm, conv_w, w_out,
              g_ffn, w_up, ffn_conv_w, w_down, g_final):
    slopes = alibi_slopes(ATT_HEADS)
    prompt_attn = functools.partial(attend_prompt, slopes=slopes)
    f32 = jnp.float32
    c0p = jnp.zeros((BATCH, MLSTM_HEADS, MLSTM_DH, MLSTM_DH), f32)
    n0p = jnp.zeros((BATCH, MLSTM_HEADS, MLSTM_DH), f32)
    m0p = jnp.full((BATCH, MLSTM_HEADS), -jnp.inf, f32)
    cbp = jnp.zeros((BATCH, CONV_W - 1, CONV_DIM), x_prompt.dtype)
    fbp = jnp.zeros((BATCH, CONV_W - 1, 2 * D_FF), x_prompt.dtype)

    xp, xs = x_prompt, x_sample
    outs_p = [[] for _ in range(7)]
    outs_s = [[] for _ in range(7)]
    for l in range(DEPTH):
        lw = (g_mix[l], w_in[l], b_if[l], att_lambda[l], att_subln[l], mlstm_norm[l], conv_w[l],
              w_out[l], g_ffn[l], w_up[l], ffn_conv_w[l], w_down[l])
        xp, *st_p = hybrid_layer(xp, prompt_attn, c0p, n0p, m0p, cbp, fbp, l, *lw)
        sample_attn = functools.partial(attend_sample, slopes=slopes, k_pool=cache_k[l],
                                        v_pool=cache_v[l], page_table=page_table)
        xs, *st_s = hybrid_layer(xs, sample_attn, state_mlstm_c[l], state_mlstm_n[l], state_mlstm_m[l],
                                 state_conv[l], state_ffn_conv[l], l, *lw)
        for i in range(7):
            outs_p[i].append(st_p[i])
            outs_s[i].append(st_s[i])

    y_prompt = rmsnorm(xp, g_final)
    y_sample = rmsnorm(xs, g_final)
    k_prompt, v_prompt, c_prompt, n_prompt, m_prompt, conv_prompt, ffn_conv_prompt = [jnp.stack(a, 0) for a in outs_p]
    k_sample, v_sample, c_sample, n_sample, m_sample, conv_sample, ffn_conv_sample = [jnp.stack(a, 0) for a in outs_s]
    return (y_prompt, y_sample, k_prompt, v_prompt, k_sample, v_sample,
            c_prompt, n_prompt, m_prompt, c_sample, n_sample, m_sample,
            conv_prompt, conv_sample, ffn_conv_prompt, ffn_conv_sample)
```

```python
import functools
import math

import numpy as np
import jax
import jax.numpy as jnp
from jax import lax
from jax.experimental import pallas as pl
from jax.experimental.pallas import tpu as pltpu

F32 = jnp.float32
BF16 = jnp.bfloat16

D_MODEL = 2048
N_HEADS_ATT = 6
ATT_DH = 64
ATT_HW = 2 * ATT_DH
ATT_W = N_HEADS_ATT * ATT_HW
N_HEADS_ML = 4
ML_DH = 192
ML_W = N_HEADS_ML * ML_DH
CONV_C = D_MODEL - ATT_W - ML_W
D_FF = 5632
PAGE = 128
EPS = 1e-6

N_MAIN = 3 * CONV_C + 3 * ATT_W + 4 * ML_W
OFF_CB, OFF_CC, OFF_CX = 0, CONV_C, 2 * CONV_C
OFF_AQ = 3 * CONV_C
OFF_AK = OFF_AQ + ATT_W
OFF_AV = OFF_AK + ATT_W
OFF_MQ = OFF_AV + ATT_W
OFF_MK = OFF_MQ + ML_W
OFF_MV = OFF_MK + ML_W
OFF_MO = OFF_MV + ML_W
GATE_W = 128

VMEM_LIMIT = 56 * 1024 * 1024


def _cparams(sem):
    return pltpu.CompilerParams(dimension_semantics=sem, vmem_limit_bytes=VMEM_LIMIT)


def _alibi_slopes(n_heads):
    def geometric(n):
        start = 2.0 ** (-8.0 / n)
        return [start ** (i + 1) for i in range(n)]
    closest = 2 ** int(math.floor(math.log2(n_heads)))
    slopes = geometric(closest)
    if closest < n_heads:
        slopes = slopes + geometric(2 * closest)[0::2][: n_heads - closest]
    return np.asarray(slopes, dtype=np.float32)


def _rms(x, g):
    return x * lax.rsqrt(jnp.mean(x * x, axis=-1, keepdims=True) + EPS) * g


def _log_sigmoid(x):
    return jnp.minimum(x, 0.0) - jnp.log1p(jnp.exp(-jnp.abs(x)))


def _sigmoid(x):
    return 1.0 / (1.0 + jnp.exp(-x))


def _lambda(al_ref, lam_init):
    al = al_ref[...]
    s01 = jnp.sum(al[0:1] * al[1:2], axis=-1, keepdims=True)
    s23 = jnp.sum(al[2:3] * al[3:4], axis=-1, keepdims=True)
    return jnp.exp(s01) - jnp.exp(s23) + lam_init


def _dot_nt(a, b):
    return lax.dot_general(a, b, (((1,), (1,)), ((), ())), preferred_element_type=F32)


def _dot_tn(a, b):
    return lax.dot_general(a, b, (((0,), (0,)), ((), ())), preferred_element_type=F32)


def _dot(a, b):
    return jnp.dot(a, b, preferred_element_type=F32)


TN = ATT_W
J_Q = OFF_AQ // TN
N_REST = N_MAIN - 3 * ATT_W
R_CB, R_CC, R_CX = 0, CONV_C, 2 * CONV_C
R_MQ = 3 * CONV_C
R_MK, R_MV, R_MO = R_MQ + ML_W, R_MQ + 2 * ML_W, R_MQ + 3 * ML_W


def _inproj_kernel(*refs, layer_slot):
    if layer_slot:
        refs = refs[2:]
    x_ref, g_ref, w_ref, wg_ref, rest_ref, og_ref, q_ref, k_ref, v_ref, h_ref = refs
    j = pl.program_id(2)

    @pl.when(j == 0)
    def _():
        h = _rms(x_ref[...], g_ref[...]).astype(BF16)
        h_ref[...] = h
        og_ref[...] = _dot(h, wg_ref[...])

    res = _dot(h_ref[...], w_ref[...])

    @pl.when((j < J_Q) | (j >= J_Q + 3))
    def _():
        rest_ref[...] = res

    for t, ref in enumerate((q_ref, k_ref, v_ref)):
        @pl.when(j == J_Q + t)
        def _(ref=ref):
            for h in range(N_HEADS_ATT):
                ref[h] = res[:, h * ATT_HW:(h + 1) * ATT_HW]


def _inproj(x, g, w_main, w_gate, n_seq, seq, tm, layer, depth, k_all=None, v_all=None):
    m = x.shape[0]
    tps = seq // tm
    nj = N_MAIN // TN
    alias = k_all is not None

    def rest_map(b, i, j):
        jr = jnp.where(j < J_Q, j, jnp.maximum(j - 3, J_Q - 1))
        return (b * tps + i, jr)

    row = lambda b, i, j: (b * tps + i, 0)
    const = lambda b, i, j: (0, 0)
    kv_spec = pl.BlockSpec((None, None, N_HEADS_ATT, tm, ATT_HW), lambda b, i, j: (layer, b, 0, i, 0))
    in_specs = [
        pl.BlockSpec((tm, D_MODEL), row),
        pl.BlockSpec((1, D_MODEL), const),
        pl.BlockSpec((D_MODEL, TN), lambda b, i, j: (0, j)),
        pl.BlockSpec((D_MODEL, GATE_W), const),
    ]
    args = [x, g, w_main, w_gate]
    if alias:
        in_specs = [pl.BlockSpec(memory_space=pl.ANY), pl.BlockSpec(memory_space=pl.ANY)] + in_specs
        args = [k_all, v_all] + args
    kv_shape = jax.ShapeDtypeStruct((depth, n_seq, N_HEADS_ATT, seq, ATT_HW), F32)
    return pl.pallas_call(
        functools.partial(_inproj_kernel, layer_slot=alias),
        grid=(n_seq, tps, nj),
        in_specs=in_specs,
        out_specs=[
            pl.BlockSpec((tm, TN), rest_map),
            pl.BlockSpec((tm, GATE_W), row),
            pl.BlockSpec((None, N_HEADS_ATT, tm, ATT_HW), lambda b, i, j: (b, 0, i, 0)),
            kv_spec, kv_spec,
        ],
        out_shape=[
            jax.ShapeDtypeStruct((m, N_REST), F32),
            jax.ShapeDtypeStruct((m, GATE_W), F32),
            jax.ShapeDtypeStruct((n_seq, N_HEADS_ATT, seq, ATT_HW), F32),
            kv_shape, kv_shape,
        ],
        scratch_shapes=[pltpu.VMEM((tm, D_MODEL), BF16)],
        input_output_aliases={0: 3, 1: 4} if alias else {},
        compiler_params=_cparams(("parallel", "parallel", "arbitrary")),
        name="inproj",
    )(*args)


def _attn_prompt_kernel(slopes_ref, qa_ref, qb_ref, k_ref, v_ref, al_ref, sub_ref, oa_ref, ob_ref,
                        q2_ref, kb_ref, vb_ref, m_ref, l_ref, acc_ref, *, tq, rs, nq, lam_init):
    h = pl.program_id(1)
    i = pl.program_id(2)
    slope = slopes_ref[h]
    tile_a, tile_b = i, nq - 1 - i

    @pl.when(i == 0)
    def _():
        kb_ref[...] = k_ref[...].astype(BF16)
        vb_ref[...] = v_ref[...].astype(BF16)

    lane = lax.broadcasted_iota(jnp.int32, (tq, ATT_HW), 1)
    for n, ref in enumerate((qa_ref, qb_ref)):
        q = ref[...] * (ATT_DH ** -0.5)
        q2_ref[2 * n * tq:(2 * n + 1) * tq, :] = jnp.where(lane < ATT_DH, q, 0.0).astype(BF16)
        q2_ref[(2 * n + 1) * tq:(2 * n + 2) * tq, :] = jnp.where(lane >= ATT_DH, q, 0.0).astype(BF16)
    m_ref[...] = jnp.full_like(m_ref, -jnp.inf)
    l_ref[...] = jnp.zeros_like(l_ref)
    acc_ref[...] = jnp.zeros_like(acc_ref)
    col = lax.broadcasted_iota(jnp.int32, (1, tq), 1)

    def block(row_base, tile, kblk, diag):
        start = pl.multiple_of(kblk * tq, tq)
        bias = slope * (col + (kblk - tile) * tq).astype(F32)
        for t in range(2 * tq // rs):
            r0 = row_base + t * rs
            rows = pl.ds(r0 if isinstance(r0, int) else pl.multiple_of(r0, rs), rs)
            q0 = (t * rs) % tq
            nk = q0 + rs if diag else tq
            k = kb_ref[pl.ds(start, nk), :]
            v = vb_ref[pl.ds(start, nk), :]
            s = _dot_nt(q2_ref[rows, :], k) + bias[:, :nk]
            if diag:
                r = lax.broadcasted_iota(jnp.int32, (rs, nk), 0) + q0
                c = lax.broadcasted_iota(jnp.int32, (rs, nk), 1)
                s = jnp.where(c <= r, s, -jnp.inf)
            m_old = m_ref[rows, :]
            m_new = jnp.maximum(m_old, jnp.max(s, axis=-1, keepdims=True))
            alpha = jnp.exp(m_old - m_new)
            p = jnp.exp(s - jnp.tile(m_new, (1, nk // ATT_HW)))
            l_ref[rows, :] = alpha * l_ref[rows, :] + jnp.sum(p, axis=-1, keepdims=True)
            acc_ref[rows, :] = alpha * acc_ref[rows, :] + _dot(p.astype(BF16), v)
            m_ref[rows, :] = m_new

    for u in range(nq - 1):
        to_b = (u >= i).astype(jnp.int32)
        block(to_b * (2 * tq), jnp.where(u >= i, tile_b, tile_a), u - to_b * i, False)
    block(0, tile_a, tile_a, True)
    block(2 * tq, tile_b, tile_b, True)

    lam = _lambda(al_ref, lam_init)
    o = acc_ref[...] / l_ref[...]
    for n, ref in enumerate((oa_ref, ob_ref)):
        on = o[2 * n * tq:(2 * n + 1) * tq] - lam * o[(2 * n + 1) * tq:(2 * n + 2) * tq]
        ref[...] = (_rms(on, sub_ref[...]) * (1.0 - lam_init)).astype(ref.dtype)


def _attn_prompt(q, k_all, v_all, layer, slopes, att_lambda, subln, n_seq, seq, tq, rs, lam_init):
    nq = seq // tq
    half = nq // 2
    w = ATT_HW
    kv_spec = pl.BlockSpec((None, None, None, seq, w), lambda b, h, i: (layer, b, h, 0, 0))
    o_shape = jax.ShapeDtypeStruct((n_seq, half, tq, ATT_W), BF16)
    lo, hi = pl.pallas_call(
        functools.partial(_attn_prompt_kernel, tq=tq, rs=rs, nq=nq, lam_init=lam_init),
        grid=(n_seq, N_HEADS_ATT, half),
        in_specs=[
            pl.BlockSpec(memory_space=pltpu.SMEM),
            pl.BlockSpec((None, None, tq, w), lambda b, h, i: (b, h, i, 0)),
            pl.BlockSpec((None, None, tq, w), lambda b, h, i: (b, h, nq - 1 - i, 0)),
            kv_spec, kv_spec,
            pl.BlockSpec((4, ATT_DH), lambda b, h, i: (0, 0)),
            pl.BlockSpec((1, w), lambda b, h, i: (0, 0)),
        ],
        out_specs=[pl.BlockSpec((None, None, tq, w), lambda b, h, i: (b, i, 0, h)),
                   pl.BlockSpec((None, None, tq, w), lambda b, h, i: (b, half - 1 - i, 0, h))],
        out_shape=[o_shape, o_shape],
        scratch_shapes=[pltpu.VMEM((4 * tq, w), BF16), pltpu.VMEM((seq, w), BF16),
                        pltpu.VMEM((seq, w), BF16), pltpu.VMEM((4 * tq, w), F32),
                        pltpu.VMEM((4 * tq, w), F32), pltpu.VMEM((4 * tq, w), F32)],
        compiler_params=_cparams(("parallel", "parallel", "arbitrary")),
        name="attn_prompt",
    )(slopes, q, q, k_all, v_all, att_lambda, subln)
    return jnp.concatenate([lo, hi], axis=1).reshape(n_seq * seq, ATT_W)


def _q_two_maps(q_row):
    r = lax.broadcasted_iota(jnp.int32, (8, ATT_HW), 0)
    c = lax.broadcasted_iota(jnp.int32, (8, ATT_HW), 1)
    keep = ((r == 0) & (c < ATT_DH)) | ((r == 1) & (c >= ATT_DH))
    return jnp.where(keep, q_row * (ATT_DH ** -0.5), 0.0)


def _scores_kernel(pt_ref, q_ref, *refs, n_pages_step):
    k_refs = refs[:n_pages_step]
    s_ref = refs[n_pages_step]
    b = pl.program_id(0)
    for h in range(N_HEADS_ATT):
        q2 = _q_two_maps(q_ref[h, pl.ds(b, 1), :]).astype(BF16)
        for i in range(n_pages_step):
            s_ref[h, :, i * PAGE:(i + 1) * PAGE] = _dot_nt(q2, k_refs[i][h].astype(BF16))


def _page_specs(layer, n_pages_step):
    def pmap(i):
        return lambda b, c, pt: (layer, pt[b, c * n_pages_step + i], 0, 0, 0)
    return [pl.BlockSpec((None, None, N_HEADS_ATT, PAGE, ATT_HW), pmap(i)) for i in range(n_pages_step)]


def _attn_scores(q, cache_kt, page_table, layer, n_pages_step):
    bd, n_pages = page_table.shape
    steps = n_pages // n_pages_step
    grid_spec = pltpu.PrefetchScalarGridSpec(
        num_scalar_prefetch=1,
        grid=(bd, steps),
        in_specs=[pl.BlockSpec((N_HEADS_ATT, bd, ATT_HW), lambda b, c, pt: (0, 0, 0))]
        + _page_specs(layer, n_pages_step),
        out_specs=pl.BlockSpec((None, N_HEADS_ATT, 8, n_pages_step * PAGE),
                               lambda b, c, pt: (b, 0, 0, c)),
    )
    return pl.pallas_call(
        functools.partial(_scores_kernel, n_pages_step=n_pages_step),
        grid_spec=grid_spec,
        out_shape=jax.ShapeDtypeStruct((bd, N_HEADS_ATT, 8, n_pages * PAGE), F32),
        compiler_params=_cparams(("parallel", "arbitrary")),
        name="attn_scores",
    )(page_table, q, *([cache_kt] * n_pages_step))


def _attn_pv_kernel(pt_ref, s_ref, q_ref, kn_ref, vn_ref, al_ref, sub_ref, *refs,
                    n_pages_step, past, lam_init, slopes):
    v_refs = refs[:n_pages_step]
    o_ref = refs[n_pages_step]
    a_ref, anew_ref, acc_ref = refs[n_pages_step + 1:]
    b = pl.program_id(0)
    c = pl.program_id(1)

    @pl.when(c == 0)
    def _():
        lam = _lambda(al_ref, lam_init)
        dist = (past - lax.broadcasted_iota(jnp.int32, (1, past), 1)).astype(F32)
        for h in range(N_HEADS_ATT):
            s = s_ref[h] - slopes[h] * dist
            qk = _q_two_maps(q_ref[h, pl.ds(b, 1), :]) * kn_ref[h, pl.ds(b, 1), :]
            s_new = jnp.sum(qk, axis=-1, keepdims=True)
            m = jnp.maximum(jnp.max(s, axis=-1, keepdims=True), s_new)
            p = jnp.exp(s - m)
            p_new = jnp.exp(s_new - m)
            l = jnp.sum(p, axis=-1, keepdims=True) + p_new
            p = p / l
            p_new = p_new / l
            a_ref[h] = jnp.broadcast_to(p[0:1] - lam * p[1:2], (8, past))
            anew_ref[h] = jnp.broadcast_to(p_new[0:1] - lam * p_new[1:2], (8, ATT_HW))
        acc_ref[...] = jnp.zeros_like(acc_ref)

    for h in range(N_HEADS_ATT):
        acc = acc_ref[h]
        for i in range(n_pages_step):
            start = pl.multiple_of((c * n_pages_step + i) * PAGE, PAGE)
            a = a_ref[h, :, pl.ds(start, PAGE)].astype(BF16)
            acc = acc + _dot(a, v_refs[i][h].astype(BF16))
        acc_ref[h] = acc

    @pl.when(c == pl.num_programs(1) - 1)
    def _():
        for h in range(N_HEADS_ATT):
            o = acc_ref[h][0:1] + anew_ref[h][0:1] * vn_ref[h, pl.ds(b, 1), :]
            o_ref[:, h * ATT_HW:(h + 1) * ATT_HW] = _rms(o, sub_ref[...]) * (1.0 - lam_init)


def _attn_pv(scores, q, k_new, v_new, att_lambda, subln, cache_vt, page_table, layer,
             n_pages_step, lam_init):
    bd, n_pages = page_table.shape
    past = n_pages * PAGE
    steps = n_pages // n_pages_step
    const2 = lambda b, c, pt: (0, 0)
    const3 = lambda b, c, pt: (0, 0, 0)
    hm = pl.BlockSpec((N_HEADS_ATT, bd, ATT_HW), const3)
    grid_spec = pltpu.PrefetchScalarGridSpec(
        num_scalar_prefetch=1,
        grid=(bd, steps),
        in_specs=[
            pl.BlockSpec((None, N_HEADS_ATT, 8, past), lambda b, c, pt: (b, 0, 0, 0)),
            hm, hm, hm,
            pl.BlockSpec((4, ATT_DH), const2),
            pl.BlockSpec((1, ATT_HW), const2),
        ] + _page_specs(layer, n_pages_step),
        out_specs=pl.BlockSpec((None, 1, ATT_W), lambda b, c, pt: (b, 0, 0)),
        scratch_shapes=[pltpu.VMEM((N_HEADS_ATT, 8, past), F32),
                        pltpu.VMEM((N_HEADS_ATT, 8, ATT_HW), F32),
                        pltpu.VMEM((N_HEADS_ATT, 8, ATT_HW), F32)],
    )
    out = pl.pallas_call(
        functools.partial(_attn_pv_kernel, n_pages_step=n_pages_step, past=past,
                          lam_init=lam_init, slopes=[float(x) for x in _alibi_slopes(N_HEADS_ATT)]),
        grid_spec=grid_spec,
        out_shape=jax.ShapeDtypeStruct((bd, 1, ATT_W), F32),
        compiler_params=_cparams(("parallel", "arbitrary")),
        name="attn_pv",
    )(page_table, scores, q, k_new, v_new, att_lambda, subln, *([cache_vt] * n_pages_step))
    return out.reshape(bd, ATT_W)


def _mlstm_chunk_kernel(q_ref, k_ref, v_ref, o_ref, g_ref, bias_ref, norm_ref, c0_ref, n0_ref,
                        m0_ref, h_ref, c_ref, n_ref, m_ref, *, L):
    @pl.when(pl.program_id(1) == 0)
    def _():
        c_ref[...] = c0_ref[...]
        n_ref[...] = n0_ref[...]
        m_ref[...] = m0_ref[...]

    g = g_ref[...] + bias_ref[...]
    g_t = g.T
    r = lax.broadcasted_iota(jnp.int32, (L, L), 0)
    c = lax.broadcasted_iota(jnp.int32, (L, L), 1)
    causal = c <= r

    for h in range(N_HEADS_ML):
        sl = slice(h * ML_DH, (h + 1) * ML_DH)
        ig_r = g_t[h:h + 1, :]
        ig_c = g[:, h:h + 1]
        lf_r = _log_sigmoid(g_t[N_HEADS_ML + h:N_HEADS_ML + h + 1, :])
        lf_c = _log_sigmoid(g[:, N_HEADS_ML + h:N_HEADS_ML + h + 1])
        b_c = jnp.sum(jnp.where(causal, lf_r, 0.0), axis=1, keepdims=True)
        b_r = jnp.sum(jnp.where(causal, 0.0, lf_c) + jnp.where(r == c, lf_c, 0.0),
                      axis=0, keepdims=True)
        m_prev = m_ref[h:h + 1, 0:1]
        d = jnp.where(causal, b_c + (ig_r - b_r), -jnp.inf)
        inter = b_c + m_prev
        m_t = jnp.maximum(inter, jnp.max(d, axis=1, keepdims=True))
        w_intra = jnp.exp(d - m_t)
        w_state = jnp.exp(inter - m_t)

        q = q_ref[:, sl]
        k = k_ref[:, sl] * (ML_DH ** -0.5)
        v = v_ref[:, sl]
        qb, kb, vb = q.astype(BF16), k.astype(BF16), v.astype(BF16)
        cst = c_ref[h]
        nst = n_ref[h:h + 1, :]
        a = w_intra * _dot_nt(qb, kb)
        num = _dot(a.astype(BF16), vb) + w_state * _dot_nt(qb, cst.astype(BF16))
        den = jnp.sum(a, axis=1, keepdims=True) + w_state * jnp.sum(q * nst, axis=1, keepdims=True)
        hh = num / jnp.maximum(jnp.abs(den), jnp.exp(-m_t))

        m_new = m_t[L - 1:L, :]
        b_last = b_c[L - 1:L, :]
        w_s = jnp.exp(b_last - b_c + ig_c - m_new)
        decay = jnp.exp(b_last + m_prev - m_new)
        c_ref[h] = decay * cst + _dot_tn((w_s * v).astype(BF16), kb)
        n_ref[h:h + 1, :] = decay * nst + jnp.sum(w_s * k, axis=0, keepdims=True)
        m_ref[h:h + 1, :] = jnp.broadcast_to(m_new, (1, 128))

        hn = _rms(hh, norm_ref[:, sl])
        h_ref[:, sl] = (_sigmoid(o_ref[:, sl]) * hn).astype(h_ref.dtype)


def _mlstm_prompt(proj, gates, b_if_pad, norm, c0, n0, m0, n_seq, seq, L):
    nc = seq // L
    blk = lambda off: pl.BlockSpec((L, ML_W), lambda b, c, off=off: (b * nc + c, off // ML_W))
    const2 = lambda b, c: (0, 0)
    return pl.pallas_call(
        functools.partial(_mlstm_chunk_kernel, L=L),
        grid=(n_seq, nc),
        in_specs=[
            blk(R_MQ), blk(R_MK), blk(R_MV), blk(R_MO),
            pl.BlockSpec((L, GATE_W), lambda b, c: (b * nc + c, 0)),
            pl.BlockSpec((1, GATE_W), const2),
            pl.BlockSpec((1, ML_W), const2),
            pl.BlockSpec((None, N_HEADS_ML, ML_DH, ML_DH), lambda b, c: (b, 0, 0, 0)),
            pl.BlockSpec((None, N_HEADS_ML, ML_DH), lambda b, c: (b, 0, 0)),
            pl.BlockSpec((None, N_HEADS_ML, 128), lambda b, c: (b, 0, 0)),
        ],
        out_specs=[
            pl.BlockSpec((L, ML_W), lambda b, c: (b * nc + c, 0)),
            pl.BlockSpec((None, N_HEADS_ML, ML_DH, ML_DH), lambda b, c: (b, 0, 0, 0)),
            pl.BlockSpec((None, N_HEADS_ML, ML_DH), lambda b, c: (b, 0, 0)),
            pl.BlockSpec((None, N_HEADS_ML, 128), lambda b, c: (b, 0, 0)),
        ],
        out_shape=[
            jax.ShapeDtypeStruct((n_seq * seq, ML_W), BF16),
            jax.ShapeDtypeStruct((n_seq, N_HEADS_ML, ML_DH, ML_DH), F32),
            jax.ShapeDtypeStruct((n_seq, N_HEADS_ML, ML_DH), F32),
            jax.ShapeDtypeStruct((n_seq, N_HEADS_ML, 128), F32),
        ],
        compiler_params=_cparams(("parallel", "arbitrary")),
        name="mlstm_prompt",
    )(proj, proj, proj, proj, gates, b_if_pad, norm, c0, n0, m0)


def _mlstm_step_kernel(q_ref, k_ref, v_ref, o_ref, g_ref, bias_ref, norm_ref, c0_ref, n0_ref,
                       m0_ref, h_ref, c_ref, n_ref, m_ref):
    b = pl.program_id(0)
    g = g_ref[pl.ds(b, 1), :] + bias_ref[...]
    r = lax.broadcasted_iota(jnp.int32, (ML_DH, ML_DH), 0)
    c = lax.broadcasted_iota(jnp.int32, (ML_DH, ML_DH), 1)
    eye = r == c
    for h in range(N_HEADS_ML):
        sl = slice(h * ML_DH, (h + 1) * ML_DH)
        ig = g[:, h:h + 1]
        lf = _log_sigmoid(g[:, N_HEADS_ML + h:N_HEADS_ML + h + 1])
        m_prev = m0_ref[h:h + 1, 0:1]
        inter = lf + m_prev
        m_t = jnp.maximum(inter, ig)
        w_i = jnp.exp(ig - m_t)
        w_state = jnp.exp(inter - m_t)

        q = q_ref[pl.ds(b, 1), sl]
        k = k_ref[pl.ds(b, 1), sl] * (ML_DH ** -0.5)
        v = v_ref[pl.ds(b, 1), sl]
        cst = c0_ref[h]
        nst = n0_ref[h:h + 1, :]
        a = w_i * jnp.sum(q * k, axis=1, keepdims=True)
        cq = _dot_nt(jnp.broadcast_to(q, (8, ML_DH)).astype(BF16), cst.astype(BF16))[0:1]
        num = a * v + w_state * cq
        den = a + w_state * jnp.sum(q * nst, axis=1, keepdims=True)
        hh = num / jnp.maximum(jnp.abs(den), jnp.exp(-m_t))

        v_col = jnp.sum(jnp.where(eye, w_i * v, 0.0), axis=1, keepdims=True)
        c_ref[h] = w_state * cst + v_col * k
        n_ref[h:h + 1, :] = w_state * nst + w_i * k
        m_ref[h:h + 1, :] = jnp.broadcast_to(m_t, (1, 128))

        hn = _rms(hh, norm_ref[:, sl])
        h_ref[:, sl] = _sigmoid(o_ref[pl.ds(b, 1), sl]) * hn


def _mlstm_step(proj, gates, b_if_pad, norm, c0, n0, m0):
    bd = proj.shape[0]
    blk = lambda off: pl.BlockSpec((bd, ML_W), lambda b, off=off: (0, off // ML_W))
    const2 = lambda b: (0, 0)
    st4 = pl.BlockSpec((None, N_HEADS_ML, ML_DH, ML_DH), lambda b: (b, 0, 0, 0))
    st3 = pl.BlockSpec((None, N_HEADS_ML, ML_DH), lambda b: (b, 0, 0))
    stm = pl.BlockSpec((None, N_HEADS_ML, 128), lambda b: (b, 0, 0))
    out = pl.pallas_call(
        _mlstm_step_kernel,
        grid=(bd,),
        in_specs=[blk(R_MQ), blk(R_MK), blk(R_MV), blk(R_MO),
                  pl.BlockSpec((bd, GATE_W), const2),
                  pl.BlockSpec((1, GATE_W), const2),
                  pl.BlockSpec((1, ML_W), const2),
                  st4, st3, stm],
        out_specs=[pl.BlockSpec((None, 1, ML_W), lambda b: (b, 0, 0)), st4, st3, stm],
        out_shape=[
            jax.ShapeDtypeStruct((bd, 1, ML_W), F32),
            jax.ShapeDtypeStruct((bd, N_HEADS_ML, ML_DH, ML_DH), F32),
            jax.ShapeDtypeStruct((bd, N_HEADS_ML, ML_DH), F32),
            jax.ShapeDtypeStruct((bd, N_HEADS_ML, 128), F32),
        ],
        compiler_params=_cparams(("parallel",)),
        name="mlstm_step",
    )(proj, proj, proj, proj, gates, b_if_pad, norm, c0, n0, m0)
    return (out[0].reshape(bd, ML_W),) + tuple(out[1:])


def _conv3_rows(u, w_ref, prev8):
    tm = u.shape[0]
    row = lax.broadcasted_iota(jnp.int32, u.shape, 0)
    p1 = prev8[7:8, :]
    p2 = prev8[6:7, :]
    u1 = jnp.where(row == 0, p1, pltpu.roll(u, 1, 0))
    u2 = jnp.where(row == 0, p2, jnp.where(row == 1, p1, pltpu.roll(u, 2, 0)))
    del tm
    return w_ref[0:1, :] * u2 + w_ref[1:2, :] * u1 + w_ref[2:3, :] * u


def _conv3_state(u, w_ref, s0, s1):
    return w_ref[0:1, :] * s0 + w_ref[1:2, :] * s1 + w_ref[2:3, :] * u


def _outproj_prompt_kernel(x_ref, a_ref, hm_ref, cb_ref, cc_ref, cx_ref, cw_ref, w_ref,
                           o_ref, st_ref, carry_ref, *, tiles_per_seq):
    i = pl.program_id(0)
    u = cc_ref[...] * cx_ref[...]
    prev8 = jnp.where(i % tiles_per_seq == 0, 0.0, carry_ref[...])
    yc = cb_ref[...] * _conv3_rows(u, cw_ref, prev8)
    tail = u[u.shape[0] - 8:, :]
    carry_ref[...] = tail
    st_ref[...] = tail
    acc = _dot(a_ref[...], w_ref[0:ATT_W, :])
    acc = acc + _dot(hm_ref[...], w_ref[ATT_W:ATT_W + ML_W, :])
    acc = acc + _dot(yc.astype(BF16), w_ref[ATT_W + ML_W:, :])
    o_ref[...] = x_ref[...] + acc


def _outproj_prompt(x, a, hm, proj, conv_w, w_out, n_seq, seq, tm):
    m = x.shape[0]
    tps = seq // tm
    cblk = lambda off: pl.BlockSpec((tm, CONV_C), lambda i, off=off: (i, off // CONV_C))
    return pl.pallas_call(
        functools.partial(_outproj_prompt_kernel, tiles_per_seq=tps),
        grid=(m // tm,),
        in_specs=[
            pl.BlockSpec((tm, D_MODEL), lambda i: (i, 0)),
            pl.BlockSpec((tm, ATT_W), lambda i: (i, 0)),
            pl.BlockSpec((tm, ML_W), lambda i: (i, 0)),
            cblk(R_CB), cblk(R_CC), cblk(R_CX),
            pl.BlockSpec((3, CONV_C), lambda i: (0, 0)),
            pl.BlockSpec((D_MODEL, D_MODEL), lambda i: (0, 0)),
        ],
        out_specs=[
            pl.BlockSpec((tm, D_MODEL), lambda i: (i, 0)),
            pl.BlockSpec((None, 8, CONV_C), lambda i: (i, 0, 0)),
        ],
        out_shape=[
            jax.ShapeDtypeStruct((m, D_MODEL), F32),
            jax.ShapeDtypeStruct((m // tm, 8, CONV_C), F32),
        ],
        scratch_shapes=[pltpu.VMEM((8, CONV_C), F32)],
        compiler_params=_cparams(("arbitrary",)),
        name="outproj_prompt",
    )(x, a, hm, proj, proj, proj, conv_w, w_out)


def _outproj_step_kernel(x_ref, a_ref, hm_ref, cb_ref, cc_ref, cx_ref, cw_ref, s0_ref, s1_ref,
                         w_ref, o_ref, u_ref):
    u = cc_ref[...] * cx_ref[...]
    u_ref[...] = u
    yc = cb_ref[...] * _conv3_state(u, cw_ref, s0_ref[...], s1_ref[...])
    acc = _dot(a_ref[...].astype(BF16), w_ref[0:ATT_W, :])
    acc = acc + _dot(hm_ref[...].astype(BF16), w_ref[ATT_W:ATT_W + ML_W, :])
    acc = acc + _dot(yc.astype(BF16), w_ref[ATT_W + ML_W:, :])
    o_ref[...] = x_ref[...] + acc


def _outproj_step(x, a, hm, proj, conv_w, s0, s1, w_out):
    bd = x.shape[0]
    full = lambda shp: pl.BlockSpec(shp, lambda i: (0, 0))
    cblk = lambda off: pl.BlockSpec((bd, CONV_C), lambda i, off=off: (0, off // CONV_C))
    return pl.pallas_call(
        _outproj_step_kernel,
        grid=(1,),
        in_specs=[full((bd, D_MODEL)), full((bd, ATT_W)), full((bd, ML_W)),
                  cblk(R_CB), cblk(R_CC), cblk(R_CX),
                  full((3, CONV_C)), full((bd, CONV_C)), full((bd, CONV_C)),
                  full((D_MODEL, D_MODEL))],
        out_specs=[full((bd, D_MODEL)), full((bd, CONV_C))],
        out_shape=[jax.ShapeDtypeStruct((bd, D_MODEL), F32),
                   jax.ShapeDtypeStruct((bd, CONV_C), F32)],
        compiler_params=_cparams(("arbitrary",)),
        name="outproj_step",
    )(x, a, hm, proj, proj, proj, conv_w, s0, s1, w_out)


def _ffn_prompt_kernel(x_ref, g_ref, wg_ref, wv_ref, cwg_ref, cwv_ref, wd_ref,
                       o_ref, stg_ref, stv_ref, h_ref, carry_ref, *, tiles_per_seq):
    i = pl.program_id(0)
    j = pl.program_id(1)

    @pl.when(j == 0)
    def _():
        x = x_ref[...]
        h_ref[...] = _rms(x, g_ref[...]).astype(BF16)
        o_ref[...] = x

    h = h_ref[...]
    ug = _dot(h, wg_ref[...])
    uv = _dot(h, wv_ref[...])
    start = i % tiles_per_seq == 0
    pg = jnp.where(start, 0.0, carry_ref[j, 0])
    pv = jnp.where(start, 0.0, carry_ref[j, 1])
    cg = _conv3_rows(ug, cwg_ref, pg)
    cv = _conv3_rows(uv, cwv_ref, pv)
    tm = ug.shape[0]
    tg = ug[tm - 8:, :]
    tv = uv[tm - 8:, :]
    carry_ref[j, 0] = tg
    carry_ref[j, 1] = tv
    stg_ref[...] = tg
    stv_ref[...] = tv
    act = (cg * _sigmoid(cg) * cv).astype(BF16)
    o_ref[...] += _dot(act, wd_ref[...])


def _ffn_prompt(x, g, w_up, ffn_conv_w, w_down, n_seq, seq, tm, tf):
    m = x.shape[0]
    tps = seq // tm
    nj = D_FF // tf
    return pl.pallas_call(
        functools.partial(_ffn_prompt_kernel, tiles_per_seq=tps),
        grid=(m // tm, nj),
        in_specs=[
            pl.BlockSpec((tm, D_MODEL), lambda i, j: (i, 0)),
            pl.BlockSpec((1, D_MODEL), lambda i, j: (0, 0)),
            pl.BlockSpec((D_MODEL, tf), lambda i, j: (0, j)),
            pl.BlockSpec((D_MODEL, tf), lambda i, j: (0, nj + j)),
            pl.BlockSpec((3, tf), lambda i, j: (0, j)),
            pl.BlockSpec((3, tf), lambda i, j: (0, nj + j)),
            pl.BlockSpec((tf, D_MODEL), lambda i, j: (j, 0)),
        ],
        out_specs=[
            pl.BlockSpec((tm, D_MODEL), lambda i, j: (i, 0)),
            pl.BlockSpec((None, 8, tf), lambda i, j: (i, 0, j)),
            pl.BlockSpec((None, 8, tf), lambda i, j: (i, 0, j)),
        ],
        out_shape=[
            jax.ShapeDtypeStruct((m, D_MODEL), F32),
            jax.ShapeDtypeStruct((m // tm, 8, D_FF), F32),
            jax.ShapeDtypeStruct((m // tm, 8, D_FF), F32),
        ],
        scratch_shapes=[pltpu.VMEM((tm, D_MODEL), BF16), pltpu.VMEM((nj, 2, 8, tf), F32)],
        compiler_params=_cparams(("arbitrary", "arbitrary")),
        name="ffn_prompt",
    )(x, g, w_up, w_up, ffn_conv_w, ffn_conv_w, w_down)


def _ffn_step_kernel(x_ref, g_ref, wg_ref, wv_ref, cwg_ref, cwv_ref, s0g_ref, s1g_ref, s0v_ref,
                     s1v_ref, wd_ref, o_ref, ug_ref, uv_ref, h_ref):
    @pl.when(pl.program_id(0) == 0)
    def _():
        x = x_ref[...]
        h_ref[...] = _rms(x, g_ref[...]).astype(BF16)
        o_ref[...] = x

    h = h_ref[...]
    ug = _dot(h, wg_ref[...])
    uv = _dot(h, wv_ref[...])
    ug_ref[...] = ug
    uv_ref[...] = uv
    cg = _conv3_state(ug, cwg_ref, s0g_ref[...], s1g_ref[...])
    cv = _conv3_state(uv, cwv_ref, s0v_ref[...], s1v_ref[...])
    act = (cg * _sigmoid(cg) * cv).astype(BF16)
    o_ref[...] += _dot(act, wd_ref[...])


def _ffn_step(x, g, w_up, ffn_conv_w, s0, s1, w_down, tf):
    bd = x.shape[0]
    nj = D_FF // tf
    gcol = lambda rows: pl.BlockSpec((rows, tf), lambda j: (0, j))
    vcol = lambda rows: pl.BlockSpec((rows, tf), lambda j: (0, nj + j))
    return pl.pallas_call(
        _ffn_step_kernel,
        grid=(nj,),
        in_specs=[
            pl.BlockSpec((bd, D_MODEL), lambda j: (0, 0)),
            pl.BlockSpec((1, D_MODEL), lambda j: (0, 0)),
            gcol(D_MODEL), vcol(D_MODEL), gcol(3), vcol(3),
            gcol(bd), gcol(bd), vcol(bd), vcol(bd),
            pl.BlockSpec((tf, D_MODEL), lambda j: (j, 0)),
        ],
        out_specs=[pl.BlockSpec((bd, D_MODEL), lambda j: (0, 0)), gcol(bd), gcol(bd)],
        out_shape=[jax.ShapeDtypeStruct((bd, D_MODEL), F32),
                   jax.ShapeDtypeStruct((bd, D_FF), F32),
                   jax.ShapeDtypeStruct((bd, D_FF), F32)],
        scratch_shapes=[pltpu.VMEM((bd, D_MODEL), BF16)],
        compiler_params=_cparams(("arbitrary",)),
        name="ffn_step",
    )(x, g, w_up, w_up, ffn_conv_w, ffn_conv_w, s0, s1, s0, s1, w_down)


def _final_norm_kernel(x_ref, g_ref, o_ref):
    o_ref[...] = _rms(x_ref[...], g_ref[...])


def _final_norm(x, g, tm):
    m = x.shape[0]
    return pl.pallas_call(
        _final_norm_kernel,
        grid=(m // tm,),
        in_specs=[pl.BlockSpec((tm, D_MODEL), lambda i: (i, 0)),
                  pl.BlockSpec((1, D_MODEL), lambda i: (0, 0))],
        out_specs=pl.BlockSpec((tm, D_MODEL), lambda i: (i, 0)),
        out_shape=jax.ShapeDtypeStruct((m, D_MODEL), F32),
        compiler_params=_cparams(("parallel",)),
        name="final_norm",
    )(x, g)


def _tiles(seq):
    tm = min(512, seq)
    tf = 512
    tq = min(256, seq)
    rs = min(128, tq)
    lc = min(256, seq)
    return tm, tf, tq, rs, lc


def kernel(x_prompt, x_sample, cache_k, cache_v, state_mlstm_c, state_mlstm_n, state_mlstm_m,
           state_conv, state_ffn_conv, page_table,
           g_mix, w_in, b_if, att_lambda, att_subln, mlstm_norm, conv_w, w_out,
           g_ffn, w_up, ffn_conv_w, w_down, g_final):
    depth = w_in.shape[0]
    n_seq, seq, _ = x_prompt.shape
    bd = x_sample.shape[0]
    n_pool = cache_k.shape[1]
    tm, tf, tq, rs, lc = _tiles(seq)
    tps = seq // tm
    n_pages = page_table.shape[1]
    pages_step = math.gcd(16, n_pages)

    n_gate = 2 * N_HEADS_ML
    g_off = 3 * ATT_W + 4 * ML_W
    w_main = jnp.concatenate([w_in[:, :, g_off + n_gate:], w_in[:, :, :g_off]], axis=-1).astype(BF16)
    w_gate = jnp.pad(w_in[:, :, g_off:g_off + n_gate], ((0, 0), (0, 0), (0, GATE_W - n_gate))).astype(BF16)
    w_out_b = w_out.astype(BF16)
    w_up_b = w_up.astype(BF16)
    w_down_b = w_down.astype(BF16)
    b_if_pad = jnp.pad(b_if, ((0, 0), (0, GATE_W - n_gate))).reshape(depth, 1, GATE_W)

    slopes = jnp.asarray(_alibi_slopes(N_HEADS_ATT))
    cache_kt = cache_k.transpose(0, 1, 3, 2, 4)
    cache_vt = cache_v.transpose(0, 1, 3, 2, 4)

    c0p = jnp.zeros((n_seq, N_HEADS_ML, ML_DH, ML_DH), F32)
    n0p = jnp.zeros((n_seq, N_HEADS_ML, ML_DH), F32)
    m0p = jnp.full((n_seq, N_HEADS_ML, 128), -jnp.inf, F32)

    xp = x_prompt.reshape(n_seq * seq, D_MODEL)
    xs = x_sample.reshape(bd, D_MODEL)
    outs_p = [[] for _ in range(7)]
    outs_s = [[] for _ in range(7)]
    k_all = v_all = None
    for l in range(depth):
        lam_init = 0.8 - 0.6 * math.exp(-0.3 * l)
        gm = g_mix[l].reshape(1, D_MODEL)
        gf = g_ffn[l].reshape(1, D_MODEL)
        sub = att_subln[l].reshape(1, ATT_HW)
        mnorm = mlstm_norm[l].reshape(1, ML_W)

        proj, gates, q, k_all, v_all = _inproj(xp, gm, w_main[l], w_gate[l], n_seq, seq, tm, l, depth,
                                               k_all, v_all)
        a = _attn_prompt(q, k_all, v_all, l, slopes, att_lambda[l], sub, n_seq, seq, tq, rs, lam_init)
        hm, c1, n1, m1 = _mlstm_prompt(proj, gates, b_if_pad[l], mnorm, c0p, n0p, m0p, n_seq, seq, lc)
        xp, cst = _outproj_prompt(xp, a, hm, proj, conv_w[l], w_out_b[l], n_seq, seq, tm)
        xp, stg, stv = _ffn_prompt(xp, gf, w_up_b[l], ffn_conv_w[l], w_down_b[l], n_seq, seq, tm, tf)
        outs_p[2].append(c1)
        outs_p[3].append(n1)
        outs_p[4].append(m1[:, :, 0])
        outs_p[5].append(cst[tps - 1::tps, 6:8, :])
        outs_p[6].append(jnp.concatenate([stg[tps - 1::tps, 6:8, :], stv[tps - 1::tps, 6:8, :]], axis=-1))

        proj_s, gates_s, q_s, k_s, v_s = _inproj(xs, gm, w_main[l], w_gate[l], 1, bd, bd, 0, 1)
        q_s, k_s, v_s = q_s[0], k_s[0, 0], v_s[0, 0]
        scores = _attn_scores(q_s, cache_kt, page_table, l, pages_step)
        a_s = _attn_pv(scores, q_s, k_s, v_s, att_lambda[l], sub, cache_vt, page_table, l,
                       pages_step, lam_init)
        m0s = jnp.broadcast_to(state_mlstm_m[l][:, :, None], (bd, N_HEADS_ML, 128))
        hm_s, c1s, n1s, m1s = _mlstm_step(proj_s, gates_s, b_if_pad[l], mnorm, state_mlstm_c[l],
                                          state_mlstm_n[l], m0s)
        sc = state_conv[l]
        xs, u_s = _outproj_step(xs, a_s, hm_s, proj_s, conv_w[l], sc[:, 0], sc[:, 1], w_out_b[l])
        sf = state_ffn_conv[l]
        xs, ug_s, uv_s = _ffn_step(xs, gf, w_up_b[l], ffn_conv_w[l], sf[:, 0], sf[:, 1], w_down_b[l], tf)
        outs_s[0].append(k_s.transpose(1, 0, 2).reshape(bd, 1, N_HEADS_ATT, ATT_HW))
        outs_s[1].append(v_s.transpose(1, 0, 2).reshape(bd, 1, N_HEADS_ATT, ATT_HW))
        outs_s[2].append(c1s)
        outs_s[3].append(n1s)
        outs_s[4].append(m1s[:, :, 0])
        outs_s[5].append(jnp.stack([sc[:, 1], u_s], axis=1))
        outs_s[6].append(jnp.stack([sf[:, 1], jnp.concatenate([ug_s, uv_s], axis=-1)], axis=1))

    gfin = g_final.reshape(1, D_MODEL)
    y_prompt = _final_norm(xp, gfin, tm).reshape(n_seq, seq, D_MODEL)
    y_sample = _final_norm(xs, gfin, bd).reshape(bd, 1, D_MODEL)
    c_p, n_p, m_p, conv_p, ffn_p = [jnp.stack(o, 0) for o in outs_p[2:]]
    k_p = k_all.transpose(0, 1, 3, 2, 4)
    v_p = v_all.transpose(0, 1, 3, 2, 4)
    k_s_, v_s_, c_s, n_s, m_s, conv_s, ffn_s = [jnp.stack(o, 0) for o in outs_s]
    return (y_prompt, y_sample, k_p, v_p, k_s_, v_s_, c_p, n_p, m_p, c_s, n_s, m_s,
            conv_p, conv_s, ffn_p, ffn_s)
```

```python
import functools
import math

import numpy as np
import jax
import jax.numpy as jnp
from jax import lax
from jax.experimental import pallas as pl
from jax.experimental.pallas import tpu as pltpu

F32 = jnp.float32
BF16 = jnp.bfloat16

D_MODEL = 2048
N_HEADS_ATT = 6
ATT_DH = 64
ATT_HW = 2 * ATT_DH
ATT_W = N_HEADS_ATT * ATT_HW
N_HEADS_ML = 4
ML_DH = 192
ML_W = N_HEADS_ML * ML_DH
CONV_C = D_MODEL - ATT_W - ML_W
D_FF = 5632
PAGE = 128
EPS = 1e-6

GATE_W = 128

VMEM_LIMIT = 56 * 1024 * 1024


def _cparams(sem):
    return pltpu.CompilerParams(dimension_semantics=sem, vmem_limit_bytes=VMEM_LIMIT)


def _alibi_slopes(n_heads):
    def geometric(n):
        start = 2.0 ** (-8.0 / n)
        return [start ** (i + 1) for i in range(n)]
    closest = 2 ** int(math.floor(math.log2(n_heads)))
    slopes = geometric(closest)
    if closest < n_heads:
        slopes = slopes + geometric(2 * closest)[0::2][: n_heads - closest]
    return np.asarray(slopes, dtype=np.float32)


def _rms(x, g):
    return x * lax.rsqrt(jnp.mean(x * x, axis=-1, keepdims=True) + EPS) * g


def _log_sigmoid(x):
    return jnp.minimum(x, 0.0) - jnp.log1p(jnp.exp(-jnp.abs(x)))


def _sigmoid(x):
    return 0.5 * (jnp.tanh(0.5 * x) + 1.0)


def _lambda(al_ref, lam_init):
    al = al_ref[...]
    s01 = jnp.sum(al[0:1] * al[1:2], axis=-1, keepdims=True)
    s23 = jnp.sum(al[2:3] * al[3:4], axis=-1, keepdims=True)
    return jnp.exp(s01) - jnp.exp(s23) + lam_init


def _dot_nt(a, b):
    return lax.dot_general(a, b, (((1,), (1,)), ((), ())), preferred_element_type=F32)


def _dot_tn(a, b):
    return lax.dot_general(a, b, (((0,), (0,)), ((), ())), preferred_element_type=F32)


def _dot(a, b):
    return jnp.dot(a, b, preferred_element_type=F32)


TN = ATT_W
N_QKV_MLSTM = 3 * ATT_W + 4 * ML_W
J_ML = 3
J_CONV = N_QKV_MLSTM // TN
N_REST = 4 * ML_W + 3 * CONV_C
R_MQ, R_MK, R_MV, R_MO = 0, ML_W, 2 * ML_W, 3 * ML_W
R_CB = 4 * ML_W
R_CC, R_CX = R_CB + CONV_C, R_CB + 2 * CONV_C


def _inproj_kernel(*refs, layer_slot):
    if layer_slot:
        refs = refs[2:]
    x_ref, g_ref, w_ref, wc_ref, wg_ref, rest_ref, og_ref, q_ref, k_ref, v_ref, h_ref = refs
    j = pl.program_id(2)

    @pl.when(j == 0)
    def _():
        h = _rms(x_ref[...], g_ref[...]).astype(BF16)
        h_ref[...] = h
        og_ref[...] = _dot(h, wg_ref[...])

    for t, ref in enumerate((q_ref, k_ref, v_ref)):
        @pl.when(j == t)
        def _(ref=ref):
            res = _dot(h_ref[...], w_ref[...])
            for h in range(N_HEADS_ATT):
                ref[h] = res[:, h * ATT_HW:(h + 1) * ATT_HW]

    @pl.when((j >= J_ML) & (j < J_CONV))
    def _():
        rest_ref[...] = _dot(h_ref[...], w_ref[...])

    @pl.when(j >= J_CONV)
    def _():
        rest_ref[...] = _dot(h_ref[...], wc_ref[...])


def _inproj(x, g, w_bf, w_conv, n_seq, seq, tm, layer, depth, k_all=None, v_all=None):
    m = x.shape[0]
    tps = seq // tm
    nj = J_CONV + 3 * CONV_C // TN
    alias = k_all is not None
    rest_map = lambda b, i, j: (b * tps + i, jnp.maximum(j - J_ML, 0))
    row = lambda b, i, j: (b * tps + i, 0)
    const = lambda b, i, j: (0, 0)
    kv_spec = pl.BlockSpec((None, None, N_HEADS_ATT, tm, ATT_HW), lambda b, i, j: (layer, b, 0, i, 0))
    in_specs = [
        pl.BlockSpec((tm, D_MODEL), row),
        pl.BlockSpec((1, D_MODEL), const),
        pl.BlockSpec((D_MODEL, TN), lambda b, i, j: (0, jnp.minimum(j, J_CONV - 1))),
        pl.BlockSpec((D_MODEL, TN), lambda b, i, j: (0, jnp.maximum(j - J_CONV, 0))),
        pl.BlockSpec((D_MODEL, GATE_W), lambda b, i, j: (0, N_QKV_MLSTM // GATE_W)),
    ]
    args = [x, g, w_bf, w_conv, w_bf]
    if alias:
        in_specs = [pl.BlockSpec(memory_space=pl.ANY), pl.BlockSpec(memory_space=pl.ANY)] + in_specs
        args = [k_all, v_all] + args
    kv_shape = jax.ShapeDtypeStruct((depth, n_seq, N_HEADS_ATT, seq, ATT_HW), F32)
    return pl.pallas_call(
        functools.partial(_inproj_kernel, layer_slot=alias),
        grid=(n_seq, tps, nj),
        in_specs=in_specs,
        out_specs=[
            pl.BlockSpec((tm, TN), rest_map),
            pl.BlockSpec((tm, GATE_W), row),
            pl.BlockSpec((None, N_HEADS_ATT, tm, ATT_HW), lambda b, i, j: (b, 0, i, 0)),
            kv_spec, kv_spec,
        ],
        out_shape=[
            jax.ShapeDtypeStruct((m, N_REST), F32),
            jax.ShapeDtypeStruct((m, GATE_W), F32),
            jax.ShapeDtypeStruct((n_seq, N_HEADS_ATT, seq, ATT_HW), F32),
            kv_shape, kv_shape,
        ],
        scratch_shapes=[pltpu.VMEM((tm, D_MODEL), BF16)],
        input_output_aliases={0: 3, 1: 4} if alias else {},
        compiler_params=_cparams(("parallel", "parallel", "arbitrary")),
        name="inproj",
    )(*args)


def _attn_prompt_kernel(slopes_ref, qa_ref, qb_ref, k_ref, v_ref, al_ref, sub_ref, oa_ref, ob_ref,
                        q2_ref, kb_ref, vb_ref, m_ref, l_ref, acc_ref, *, tq, rs, nq, lam_init):
    h = pl.program_id(1)
    i = pl.program_id(2)
    slope = slopes_ref[h]
    tile_a, tile_b = i, nq - 1 - i

    @pl.when(i == 0)
    def _():
        kb_ref[...] = k_ref[...].astype(BF16)
        vb_ref[...] = v_ref[...].astype(BF16)

    lane = lax.broadcasted_iota(jnp.int32, (tq, ATT_HW), 1)
    for n, ref in enumerate((qa_ref, qb_ref)):
        q = ref[...] * (ATT_DH ** -0.5)
        q2_ref[2 * n * tq:(2 * n + 1) * tq, :] = jnp.where(lane < ATT_DH, q, 0.0).astype(BF16)
        q2_ref[(2 * n + 1) * tq:(2 * n + 2) * tq, :] = jnp.where(lane >= ATT_DH, q, 0.0).astype(BF16)
    m_ref[...] = jnp.full_like(m_ref, -jnp.inf)
    l_ref[...] = jnp.zeros_like(l_ref)
    acc_ref[...] = jnp.zeros_like(acc_ref)
    col = lax.broadcasted_iota(jnp.int32, (1, tq), 1)

    def block(row_base, tile, kblk, diag):
        start = pl.multiple_of(kblk * tq, tq)
        bias = slope * (col + (kblk - tile) * tq).astype(F32)
        for t in range(2 * tq // rs):
            r0 = row_base + t * rs
            rows = pl.ds(r0 if isinstance(r0, int) else pl.multiple_of(r0, rs), rs)
            q0 = (t * rs) % tq
            nk = min(tq, -(-(q0 + rs) // ATT_HW) * ATT_HW) if diag else tq
            k = kb_ref[pl.ds(start, nk), :]
            v = vb_ref[pl.ds(start, nk), :]
            s = _dot_nt(q2_ref[rows, :], k) + bias[:, :nk]
            if diag:
                r = lax.broadcasted_iota(jnp.int32, (rs, nk), 0) + q0
                c = lax.broadcasted_iota(jnp.int32, (rs, nk), 1)
                s = jnp.where(c <= r, s, -jnp.inf)
            m_old = m_ref[rows, :]
            m_new = jnp.maximum(m_old, jnp.max(s, axis=-1, keepdims=True))
            alpha = jnp.exp(m_old - m_new)
            p = jnp.exp(s - jnp.tile(m_new, (1, nk // ATT_HW)))
            l_ref[rows, :] = alpha * l_ref[rows, :] + jnp.sum(p, axis=-1, keepdims=True)
            acc_ref[rows, :] = alpha * acc_ref[rows, :] + _dot(p.astype(BF16), v)
            m_ref[rows, :] = m_new

    for u in range(nq - 1):
        to_b = (u >= i).astype(jnp.int32)
        block(to_b * (2 * tq), jnp.where(u >= i, tile_b, tile_a), u - to_b * i, False)
    block(0, tile_a, tile_a, True)
    block(2 * tq, tile_b, tile_b, True)

    lam = _lambda(al_ref, lam_init)
    o = acc_ref[...] / l_ref[...]
    for n, ref in enumerate((oa_ref, ob_ref)):
        on = o[2 * n * tq:(2 * n + 1) * tq] - lam * o[(2 * n + 1) * tq:(2 * n + 2) * tq]
        ref[...] = (_rms(on, sub_ref[...]) * (1.0 - lam_init)).astype(ref.dtype)


def _attn_prompt(q, k_all, v_all, layer, slopes, att_lambda, subln, n_seq, seq, tq, rs, lam_init):
    nq = seq // tq
    half = nq // 2
    w = ATT_HW
    kv_spec = pl.BlockSpec((None, None, None, seq, w), lambda b, h, i: (layer, b, h, 0, 0))
    o_shape = jax.ShapeDtypeStruct((n_seq, half, tq, ATT_W), BF16)
    lo, hi = pl.pallas_call(
        functools.partial(_attn_prompt_kernel, tq=tq, rs=rs, nq=nq, lam_init=lam_init),
        grid=(n_seq, N_HEADS_ATT, half),
        in_specs=[
            pl.BlockSpec(memory_space=pltpu.SMEM),
            pl.BlockSpec((None, None, tq, w), lambda b, h, i: (b, h, i, 0)),
            pl.BlockSpec((None, None, tq, w), lambda b, h, i: (b, h, nq - 1 - i, 0)),
            kv_spec, kv_spec,
            pl.BlockSpec((4, ATT_DH), lambda b, h, i: (0, 0)),
            pl.BlockSpec((1, w), lambda b, h, i: (0, 0)),
        ],
        out_specs=[pl.BlockSpec((None, None, tq, w), lambda b, h, i: (b, i, 0, h)),
                   pl.BlockSpec((None, None, tq, w), lambda b, h, i: (b, half - 1 - i, 0, h))],
        out_shape=[o_shape, o_shape],
        scratch_shapes=[pltpu.VMEM((4 * tq, w), BF16), pltpu.VMEM((seq, w), BF16),
                        pltpu.VMEM((seq, w), BF16), pltpu.VMEM((4 * tq, w), F32),
                        pltpu.VMEM((4 * tq, w), F32), pltpu.VMEM((4 * tq, w), F32)],
        compiler_params=_cparams(("parallel", "parallel", "arbitrary")),
        name="attn_prompt",
    )(slopes, q, q, k_all, v_all, att_lambda, subln)
    return jnp.concatenate([lo, hi], axis=1).reshape(n_seq * seq, ATT_W)


def _q_two_maps(q_row):
    r = lax.broadcasted_iota(jnp.int32, (8, ATT_HW), 0)
    c = lax.broadcasted_iota(jnp.int32, (8, ATT_HW), 1)
    keep = ((r == 0) & (c < ATT_DH)) | ((r == 1) & (c >= ATT_DH))
    return jnp.where(keep, q_row * (ATT_DH ** -0.5), 0.0)


def _scores_kernel(pt_ref, q_ref, *refs, n_pages_step):
    k_refs = refs[:n_pages_step]
    s_ref = refs[n_pages_step]
    b = pl.program_id(0)
    for h in range(N_HEADS_ATT):
        q2 = _q_two_maps(q_ref[h, pl.ds(b, 1), :]).astype(BF16)
        for i in range(n_pages_step):
            s_ref[h, :, i * PAGE:(i + 1) * PAGE] = _dot_nt(q2, k_refs[i][h].astype(BF16))


def _page_specs(layer, n_pages_step):
    def pmap(i):
        return lambda b, c, pt: (layer, pt[b, c * n_pages_step + i], 0, 0, 0)
    return [pl.BlockSpec((None, None, N_HEADS_ATT, PAGE, ATT_HW), pmap(i)) for i in range(n_pages_step)]


def _attn_scores(q, cache_kt, page_table, layer, n_pages_step):
    bd, n_pages = page_table.shape
    steps = n_pages // n_pages_step
    grid_spec = pltpu.PrefetchScalarGridSpec(
        num_scalar_prefetch=1,
        grid=(bd, steps),
        in_specs=[pl.BlockSpec((N_HEADS_ATT, bd, ATT_HW), lambda b, c, pt: (0, 0, 0))]
        + _page_specs(layer, n_pages_step),
        out_specs=pl.BlockSpec((None, N_HEADS_ATT, 8, n_pages_step * PAGE),
                               lambda b, c, pt: (b, 0, 0, c)),
    )
    return pl.pallas_call(
        functools.partial(_scores_kernel, n_pages_step=n_pages_step),
        grid_spec=grid_spec,
        out_shape=jax.ShapeDtypeStruct((bd, N_HEADS_ATT, 8, n_pages * PAGE), F32),
        compiler_params=_cparams(("parallel", "arbitrary")),
        name="attn_scores",
    )(page_table, q, *([cache_kt] * n_pages_step))


def _attn_pv_kernel(pt_ref, s_ref, q_ref, kn_ref, vn_ref, al_ref, sub_ref, *refs,
                    n_pages_step, past, lam_init, slopes):
    v_refs = refs[:n_pages_step]
    o_ref = refs[n_pages_step]
    a_ref, anew_ref, acc_ref = refs[n_pages_step + 1:]
    b = pl.program_id(0)
    c = pl.program_id(1)

    @pl.when(c == 0)
    def _():
        lam = _lambda(al_ref, lam_init)
        dist = (past - lax.broadcasted_iota(jnp.int32, (1, past), 1)).astype(F32)
        for h in range(N_HEADS_ATT):
            s = s_ref[h] - slopes[h] * dist
            qk = _q_two_maps(q_ref[h, pl.ds(b, 1), :]) * kn_ref[h, pl.ds(b, 1), :]
            s_new = jnp.sum(qk, axis=-1, keepdims=True)
            m = jnp.maximum(jnp.max(s, axis=-1, keepdims=True), s_new)
            p = jnp.exp(s - m)
            p_new = jnp.exp(s_new - m)
            l = jnp.sum(p, axis=-1, keepdims=True) + p_new
            p = p / l
            p_new = p_new / l
            a_ref[h] = jnp.broadcast_to(p[0:1] - lam * p[1:2], (8, past))
            anew_ref[h] = jnp.broadcast_to(p_new[0:1] - lam * p_new[1:2], (8, ATT_HW))
        acc_ref[...] = jnp.zeros_like(acc_ref)

    for h in range(N_HEADS_ATT):
        acc = acc_ref[h]
        for i in range(n_pages_step):
            start = pl.multiple_of((c * n_pages_step + i) * PAGE, PAGE)
            a = a_ref[h, :, pl.ds(start, PAGE)].astype(BF16)
            acc = acc + _dot(a, v_refs[i][h].astype(BF16))
        acc_ref[h] = acc

    @pl.when(c == pl.num_programs(1) - 1)
    def _():
        for h in range(N_HEADS_ATT):
            o = acc_ref[h][0:1] + anew_ref[h][0:1] * vn_ref[h, pl.ds(b, 1), :]
            o_ref[:, h * ATT_HW:(h + 1) * ATT_HW] = _rms(o, sub_ref[...]) * (1.0 - lam_init)


def _attn_pv(scores, q, k_new, v_new, att_lambda, subln, cache_vt, page_table, layer,
             n_pages_step, lam_init):
    bd, n_pages = page_table.shape
    past = n_pages * PAGE
    steps = n_pages // n_pages_step
    const2 = lambda b, c, pt: (0, 0)
    const3 = lambda b, c, pt: (0, 0, 0)
    hm = pl.BlockSpec((N_HEADS_ATT, bd, ATT_HW), const3)
    grid_spec = pltpu.PrefetchScalarGridSpec(
        num_scalar_prefetch=1,
        grid=(bd, steps),
        in_specs=[
            pl.BlockSpec((None, N_HEADS_ATT, 8, past), lambda b, c, pt: (b, 0, 0, 0)),
            hm, hm, hm,
            pl.BlockSpec((4, ATT_DH), const2),
            pl.BlockSpec((1, ATT_HW), const2),
        ] + _page_specs(layer, n_pages_step),
        out_specs=pl.BlockSpec((None, 1, ATT_W), lambda b, c, pt: (b, 0, 0)),
        scratch_shapes=[pltpu.VMEM((N_HEADS_ATT, 8, past), F32),
                        pltpu.VMEM((N_HEADS_ATT, 8, ATT_HW), F32),
                        pltpu.VMEM((N_HEADS_ATT, 8, ATT_HW), F32)],
    )
    out = pl.pallas_call(
        functools.partial(_attn_pv_kernel, n_pages_step=n_pages_step, past=past,
                          lam_init=lam_init, slopes=[float(x) for x in _alibi_slopes(N_HEADS_ATT)]),
        grid_spec=grid_spec,
        out_shape=jax.ShapeDtypeStruct((bd, 1, ATT_W), F32),
        compiler_params=_cparams(("parallel", "arbitrary")),
        name="attn_pv",
    )(page_table, scores, q, k_new, v_new, att_lambda, subln, *([cache_vt] * n_pages_step))
    return out.reshape(bd, ATT_W)


def _mlstm_chunk_kernel(q_ref, k_ref, v_ref, o_ref, g_ref, bias_ref, norm_ref, c0_ref, n0_ref,
                        m0_ref, h_ref, c_ref, n_ref, m_ref, *, L):
    @pl.when(pl.program_id(1) == 0)
    def _():
        c_ref[...] = c0_ref[...]
        n_ref[...] = n0_ref[...]
        m_ref[...] = m0_ref[...]

    g = g_ref[...] + bias_ref[...]
    g_t = g.T
    r = lax.broadcasted_iota(jnp.int32, (L, L), 0)
    c = lax.broadcasted_iota(jnp.int32, (L, L), 1)
    causal = c <= r

    for h in range(N_HEADS_ML):
        sl = slice(h * ML_DH, (h + 1) * ML_DH)
        ig_r = g_t[h:h + 1, :]
        ig_c = g[:, h:h + 1]
        lf_r = _log_sigmoid(g_t[N_HEADS_ML + h:N_HEADS_ML + h + 1, :])
        lf_c = _log_sigmoid(g[:, N_HEADS_ML + h:N_HEADS_ML + h + 1])
        b_c = jnp.sum(jnp.where(causal, lf_r, 0.0), axis=1, keepdims=True)
        b_r = jnp.sum(jnp.where(causal, 0.0, lf_c) + jnp.where(r == c, lf_c, 0.0),
                      axis=0, keepdims=True)
        m_prev = m_ref[h:h + 1, 0:1]
        d = jnp.where(causal, b_c + (ig_r - b_r), -jnp.inf)
        inter = b_c + m_prev
        m_t = jnp.maximum(inter, jnp.max(d, axis=1, keepdims=True))
        w_intra = jnp.exp(d - m_t)
        w_state = jnp.exp(inter - m_t)

        q = q_ref[:, sl]
        k = k_ref[:, sl] * (ML_DH ** -0.5)
        v = v_ref[:, sl]
        qb, kb, vb = q.astype(BF16), k.astype(BF16), v.astype(BF16)
        cst = c_ref[h]
        nst = n_ref[h:h + 1, :]
        a = w_intra * _dot_nt(qb, kb)
        num = _dot(a.astype(BF16), vb) + w_state * _dot_nt(qb, cst.astype(BF16))
        den = jnp.sum(a, axis=1, keepdims=True) + w_state * jnp.sum(q * nst, axis=1, keepdims=True)
        hh = num / jnp.maximum(jnp.abs(den), jnp.exp(-m_t))

        m_new = m_t[L - 1:L, :]
        b_last = b_c[L - 1:L, :]
        w_s = jnp.exp(b_last - b_c + ig_c - m_new)
        decay = jnp.exp(b_last + m_prev - m_new)
        c_ref[h] = decay * cst + _dot_tn((w_s * v).astype(BF16), kb)
        n_ref[h:h + 1, :] = decay * nst + jnp.sum(w_s * k, axis=0, keepdims=True)
        m_ref[h:h + 1, :] = jnp.broadcast_to(m_new, (1, 128))

        hn = _rms(hh, norm_ref[:, sl])
        h_ref[:, sl] = (_sigmoid(o_ref[:, sl]) * hn).astype(h_ref.dtype)


def _mlstm_prompt(proj, gates, b_if_pad, norm, c0, n0, m0, n_seq, seq, L):
    nc = seq // L
    blk = lambda off: pl.BlockSpec((L, ML_W), lambda b, c, off=off: (b * nc + c, off // ML_W))
    const2 = lambda b, c: (0, 0)
    return pl.pallas_call(
        functools.partial(_mlstm_chunk_kernel, L=L),
        grid=(n_seq, nc),
        in_specs=[
            blk(R_MQ), blk(R_MK), blk(R_MV), blk(R_MO),
            pl.BlockSpec((L, GATE_W), lambda b, c: (b * nc + c, 0)),
            pl.BlockSpec((1, GATE_W), const2),
            pl.BlockSpec((1, ML_W), const2),
            pl.BlockSpec((None, N_HEADS_ML, ML_DH, ML_DH), lambda b, c: (b, 0, 0, 0)),
            pl.BlockSpec((None, N_HEADS_ML, ML_DH), lambda b, c: (b, 0, 0)),
            pl.BlockSpec((None, N_HEADS_ML, 128), lambda b, c: (b, 0, 0)),
        ],
        out_specs=[
            pl.BlockSpec((L, ML_W), lambda b, c: (b * nc + c, 0)),
            pl.BlockSpec((None, N_HEADS_ML, ML_DH, ML_DH), lambda b, c: (b, 0, 0, 0)),
            pl.BlockSpec((None, N_HEADS_ML, ML_DH), lambda b, c: (b, 0, 0)),
            pl.BlockSpec((None, N_HEADS_ML, 128), lambda b, c: (b, 0, 0)),
        ],
        out_shape=[
            jax.ShapeDtypeStruct((n_seq * seq, ML_W), BF16),
            jax.ShapeDtypeStruct((n_seq, N_HEADS_ML, ML_DH, ML_DH), F32),
            jax.ShapeDtypeStruct((n_seq, N_HEADS_ML, ML_DH), F32),
            jax.ShapeDtypeStruct((n_seq, N_HEADS_ML, 128), F32),
        ],
        compiler_params=_cparams(("parallel", "arbitrary")),
        name="mlstm_prompt",
    )(proj, proj, proj, proj, gates, b_if_pad, norm, c0, n0, m0)


def _mlstm_step_kernel(q_ref, k_ref, v_ref, o_ref, g_ref, bias_ref, norm_ref, c0_ref, n0_ref,
                       m0_ref, h_ref, c_ref, n_ref, m_ref):
    b = pl.program_id(0)
    g = g_ref[pl.ds(b, 1), :] + bias_ref[...]
    r = lax.broadcasted_iota(jnp.int32, (ML_DH, ML_DH), 0)
    c = lax.broadcasted_iota(jnp.int32, (ML_DH, ML_DH), 1)
    eye = r == c
    for h in range(N_HEADS_ML):
        sl = slice(h * ML_DH, (h + 1) * ML_DH)
        ig = g[:, h:h + 1]
        lf = _log_sigmoid(g[:, N_HEADS_ML + h:N_HEADS_ML + h + 1])
        m_prev = m0_ref[h:h + 1, 0:1]
        inter = lf + m_prev
        m_t = jnp.maximum(inter, ig)
        w_i = jnp.exp(ig - m_t)
        w_state = jnp.exp(inter - m_t)

        q = q_ref[pl.ds(b, 1), sl]
        k = k_ref[pl.ds(b, 1), sl] * (ML_DH ** -0.5)
        v = v_ref[pl.ds(b, 1), sl]
        cst = c0_ref[h]
        nst = n0_ref[h:h + 1, :]
        a = w_i * jnp.sum(q * k, axis=1, keepdims=True)
        cq = _dot_nt(jnp.broadcast_to(q, (8, ML_DH)).astype(BF16), cst.astype(BF16))[0:1]
        num = a * v + w_state * cq
        den = a + w_state * jnp.sum(q * nst, axis=1, keepdims=True)
        hh = num / jnp.maximum(jnp.abs(den), jnp.exp(-m_t))

        v_col = jnp.sum(jnp.where(eye, w_i * v, 0.0), axis=1, keepdims=True)
        c_ref[h] = w_state * cst + v_col * k
        n_ref[h:h + 1, :] = w_state * nst + w_i * k
        m_ref[h:h + 1, :] = jnp.broadcast_to(m_t, (1, 128))

        hn = _rms(hh, norm_ref[:, sl])
        h_ref[:, sl] = _sigmoid(o_ref[pl.ds(b, 1), sl]) * hn


def _mlstm_step(proj, gates, b_if_pad, norm, c0, n0, m0):
    bd = proj.shape[0]
    blk = lambda off: pl.BlockSpec((bd, ML_W), lambda b, off=off: (0, off // ML_W))
    const2 = lambda b: (0, 0)
    st4 = pl.BlockSpec((None, N_HEADS_ML, ML_DH, ML_DH), lambda b: (b, 0, 0, 0))
    st3 = pl.BlockSpec((None, N_HEADS_ML, ML_DH), lambda b: (b, 0, 0))
    stm = pl.BlockSpec((None, N_HEADS_ML, 128), lambda b: (b, 0, 0))
    out = pl.pallas_call(
        _mlstm_step_kernel,
        grid=(bd,),
        in_specs=[blk(R_MQ), blk(R_MK), blk(R_MV), blk(R_MO),
                  pl.BlockSpec((bd, GATE_W), const2),
                  pl.BlockSpec((1, GATE_W), const2),
                  pl.BlockSpec((1, ML_W), const2),
                  st4, st3, stm],
        out_specs=[pl.BlockSpec((None, 1, ML_W), lambda b: (b, 0, 0)), st4, st3, stm],
        out_shape=[
            jax.ShapeDtypeStruct((bd, 1, ML_W), F32),
            jax.ShapeDtypeStruct((bd, N_HEADS_ML, ML_DH, ML_DH), F32),
            jax.ShapeDtypeStruct((bd, N_HEADS_ML, ML_DH), F32),
            jax.ShapeDtypeStruct((bd, N_HEADS_ML, 128), F32),
        ],
        compiler_params=_cparams(("parallel",)),
        name="mlstm_step",
    )(proj, proj, proj, proj, gates, b_if_pad, norm, c0, n0, m0)
    return (out[0].reshape(bd, ML_W),) + tuple(out[1:])


def _conv3_rows(u, w_ref, prev8):
    w0, w1, w2 = w_ref[0:1, :], w_ref[1:2, :], w_ref[2:3, :]
    y = w0 * pltpu.roll(u, 2, 0) + w1 * pltpu.roll(u, 1, 0) + w2 * u
    head = u[0:8, :]
    row = lax.broadcasted_iota(jnp.int32, head.shape, 0)
    p1 = prev8[7:8, :]
    p2 = prev8[6:7, :]
    h1 = jnp.where(row == 0, p1, pltpu.roll(head, 1, 0))
    h2 = jnp.where(row == 0, p2, jnp.where(row == 1, p1, pltpu.roll(head, 2, 0)))
    return jnp.concatenate([w0 * h2 + w1 * h1 + w2 * head, y[8:, :]], axis=0)


def _conv3_state(u, w_ref, s0, s1):
    return w_ref[0:1, :] * s0 + w_ref[1:2, :] * s1 + w_ref[2:3, :] * u


def _outproj_prompt_kernel(x_ref, a_ref, hm_ref, cb_ref, cc_ref, cx_ref, cw_ref, w_ref,
                           o_ref, st_ref, carry_ref, *, tiles_per_seq):
    i = pl.program_id(0)
    u = cc_ref[...] * cx_ref[...]
    prev8 = jnp.where(i % tiles_per_seq == 0, 0.0, carry_ref[...])
    yc = cb_ref[...] * _conv3_rows(u, cw_ref, prev8)
    tail = u[u.shape[0] - 8:, :]
    carry_ref[...] = tail
    st_ref[...] = tail
    acc = _dot(a_ref[...], w_ref[0:ATT_W, :])
    acc = acc + _dot(hm_ref[...], w_ref[ATT_W:ATT_W + ML_W, :])
    acc = acc + _dot(yc.astype(BF16), w_ref[ATT_W + ML_W:, :])
    o_ref[...] = x_ref[...] + acc


def _outproj_prompt(x, a, hm, proj, conv_w, w_out, n_seq, seq, tm):
    m = x.shape[0]
    tps = seq // tm
    cblk = lambda off: pl.BlockSpec((tm, CONV_C), lambda i, off=off: (i, off // CONV_C))
    return pl.pallas_call(
        functools.partial(_outproj_prompt_kernel, tiles_per_seq=tps),
        grid=(m // tm,),
        in_specs=[
            pl.BlockSpec((tm, D_MODEL), lambda i: (i, 0)),
            pl.BlockSpec((tm, ATT_W), lambda i: (i, 0)),
            pl.BlockSpec((tm, ML_W), lambda i: (i, 0)),
            cblk(R_CB), cblk(R_CC), cblk(R_CX),
            pl.BlockSpec((3, CONV_C), lambda i: (0, 0)),
            pl.BlockSpec((D_MODEL, D_MODEL), lambda i: (0, 0)),
        ],
        out_specs=[
            pl.BlockSpec((tm, D_MODEL), lambda i: (i, 0)),
            pl.BlockSpec((None, 8, CONV_C), lambda i: (i, 0, 0)),
        ],
        out_shape=[
            jax.ShapeDtypeStruct((m, D_MODEL), F32),
            jax.ShapeDtypeStruct((m // tm, 8, CONV_C), F32),
        ],
        scratch_shapes=[pltpu.VMEM((8, CONV_C), F32)],
        compiler_params=_cparams(("arbitrary",)),
        name="outproj_prompt",
    )(x, a, hm, proj, proj, proj, conv_w, w_out)


def _outproj_step_kernel(x_ref, a_ref, hm_ref, cb_ref, cc_ref, cx_ref, cw_ref, s0_ref, s1_ref,
                         w_ref, o_ref, u_ref):
    u = cc_ref[...] * cx_ref[...]
    u_ref[...] = u
    yc = cb_ref[...] * _conv3_state(u, cw_ref, s0_ref[...], s1_ref[...])
    acc = _dot(a_ref[...].astype(BF16), w_ref[0:ATT_W, :])
    acc = acc + _dot(hm_ref[...].astype(BF16), w_ref[ATT_W:ATT_W + ML_W, :])
    acc = acc + _dot(yc.astype(BF16), w_ref[ATT_W + ML_W:, :])
    o_ref[...] = x_ref[...] + acc


def _outproj_step(x, a, hm, proj, conv_w, s0, s1, w_out):
    bd = x.shape[0]
    full = lambda shp: pl.BlockSpec(shp, lambda i: (0, 0))
    cblk = lambda off: pl.BlockSpec((bd, CONV_C), lambda i, off=off: (0, off // CONV_C))
    return pl.pallas_call(
        _outproj_step_kernel,
        grid=(1,),
        in_specs=[full((bd, D_MODEL)), full((bd, ATT_W)), full((bd, ML_W)),
                  cblk(R_CB), cblk(R_CC), cblk(R_CX),
                  full((3, CONV_C)), full((bd, CONV_C)), full((bd, CONV_C)),
                  full((D_MODEL, D_MODEL))],
        out_specs=[full((bd, D_MODEL)), full((bd, CONV_C))],
        out_shape=[jax.ShapeDtypeStruct((bd, D_MODEL), F32),
                   jax.ShapeDtypeStruct((bd, CONV_C), F32)],
        compiler_params=_cparams(("arbitrary",)),
        name="outproj_step",
    )(x, a, hm, proj, proj, proj, conv_w, s0, s1, w_out)


def _ffn_prompt_kernel(x_ref, g_ref, wg_ref, wv_ref, cwg_ref, cwv_ref, wd_ref, gfin_ref,
                       o_ref, stg_ref, stv_ref, h_ref, carry_ref, *, tiles_per_seq, final_norm):
    i = pl.program_id(0)
    j = pl.program_id(1)

    @pl.when(j == 0)
    def _():
        x = x_ref[...]
        h_ref[...] = _rms(x, g_ref[...]).astype(BF16)
        o_ref[...] = x

    h = h_ref[...]
    ug = _dot(h, wg_ref[...])
    uv = _dot(h, wv_ref[...])
    start = i % tiles_per_seq == 0
    pg = jnp.where(start, 0.0, carry_ref[j, 0])
    pv = jnp.where(start, 0.0, carry_ref[j, 1])
    cg = _conv3_rows(ug, cwg_ref, pg)
    cv = _conv3_rows(uv, cwv_ref, pv)
    tm = ug.shape[0]
    tg = ug[tm - 8:, :]
    tv = uv[tm - 8:, :]
    carry_ref[j, 0] = tg
    carry_ref[j, 1] = tv
    stg_ref[...] = tg
    stv_ref[...] = tv
    act = (cg * _sigmoid(cg) * cv).astype(BF16)
    o_ref[...] += _dot(act, wd_ref[...])
    if final_norm:
        @pl.when(j == pl.num_programs(1) - 1)
        def _():
            o_ref[...] = _rms(o_ref[...], gfin_ref[...])


def _ffn_prompt(x, g, w_up, ffn_conv_w, w_down, g_final, final_norm, n_seq, seq, tm, tf):
    m = x.shape[0]
    tps = seq // tm
    nj = D_FF // tf
    return pl.pallas_call(
        functools.partial(_ffn_prompt_kernel, tiles_per_seq=tps, final_norm=final_norm),
        grid=(m // tm, nj),
        in_specs=[
            pl.BlockSpec((tm, D_MODEL), lambda i, j: (i, 0)),
            pl.BlockSpec((1, D_MODEL), lambda i, j: (0, 0)),
            pl.BlockSpec((D_MODEL, tf), lambda i, j: (0, j)),
            pl.BlockSpec((D_MODEL, tf), lambda i, j: (0, nj + j)),
            pl.BlockSpec((3, tf), lambda i, j: (0, j)),
            pl.BlockSpec((3, tf), lambda i, j: (0, nj + j)),
            pl.BlockSpec((tf, D_MODEL), lambda i, j: (j, 0)),
            pl.BlockSpec((1, D_MODEL), lambda i, j: (0, 0)),
        ],
        out_specs=[
            pl.BlockSpec((tm, D_MODEL), lambda i, j: (i, 0)),
            pl.BlockSpec((None, 8, tf), lambda i, j: (i, 0, j)),
            pl.BlockSpec((None, 8, tf), lambda i, j: (i, 0, j)),
        ],
        out_shape=[
            jax.ShapeDtypeStruct((m, D_MODEL), F32),
            jax.ShapeDtypeStruct((m // tm, 8, D_FF), F32),
            jax.ShapeDtypeStruct((m // tm, 8, D_FF), F32),
        ],
        scratch_shapes=[pltpu.VMEM((tm, D_MODEL), BF16), pltpu.VMEM((nj, 2, 8, tf), F32)],
        compiler_params=_cparams(("arbitrary", "arbitrary")),
        name="ffn_prompt",
    )(x, g, w_up, w_up, ffn_conv_w, ffn_conv_w, w_down, g_final)


def _ffn_step_kernel(x_ref, g_ref, wg_ref, wv_ref, cwg_ref, cwv_ref, s0g_ref, s1g_ref, s0v_ref,
                     s1v_ref, wd_ref, gfin_ref, o_ref, ug_ref, uv_ref, h_ref, *, final_norm):
    @pl.when(pl.program_id(0) == 0)
    def _():
        x = x_ref[...]
        h_ref[...] = _rms(x, g_ref[...]).astype(BF16)
        o_ref[...] = x

    h = h_ref[...]
    ug = _dot(h, wg_ref[...])
    uv = _dot(h, wv_ref[...])
    ug_ref[...] = ug
    uv_ref[...] = uv
    cg = _conv3_state(ug, cwg_ref, s0g_ref[...], s1g_ref[...])
    cv = _conv3_state(uv, cwv_ref, s0v_ref[...], s1v_ref[...])
    act = (cg * _sigmoid(cg) * cv).astype(BF16)
    o_ref[...] += _dot(act, wd_ref[...])
    if final_norm:
        @pl.when(pl.program_id(0) == pl.num_programs(0) - 1)
        def _():
            o_ref[...] = _rms(o_ref[...], gfin_ref[...])


def _ffn_step(x, g, w_up, ffn_conv_w, s0, s1, w_down, g_final, final_norm, tf):
    bd = x.shape[0]
    nj = D_FF // tf
    gcol = lambda rows: pl.BlockSpec((rows, tf), lambda j: (0, j))
    vcol = lambda rows: pl.BlockSpec((rows, tf), lambda j: (0, nj + j))
    return pl.pallas_call(
        functools.partial(_ffn_step_kernel, final_norm=final_norm),
        grid=(nj,),
        in_specs=[
            pl.BlockSpec((bd, D_MODEL), lambda j: (0, 0)),
            pl.BlockSpec((1, D_MODEL), lambda j: (0, 0)),
            gcol(D_MODEL), vcol(D_MODEL), gcol(3), vcol(3),
            gcol(bd), gcol(bd), vcol(bd), vcol(bd),
            pl.BlockSpec((tf, D_MODEL), lambda j: (j, 0)),
            pl.BlockSpec((1, D_MODEL), lambda j: (0, 0)),
        ],
        out_specs=[pl.BlockSpec((bd, D_MODEL), lambda j: (0, 0)), gcol(bd), gcol(bd)],
        out_shape=[jax.ShapeDtypeStruct((bd, D_MODEL), F32),
                   jax.ShapeDtypeStruct((bd, D_FF), F32),
                   jax.ShapeDtypeStruct((bd, D_FF), F32)],
        scratch_shapes=[pltpu.VMEM((bd, D_MODEL), BF16)],
        compiler_params=_cparams(("arbitrary",)),
        name="ffn_step",
    )(x, g, w_up, w_up, ffn_conv_w, ffn_conv_w, s0, s1, s0, s1, w_down, g_final)


def _tiles(seq):
    tm = min(512, seq)
    tf = 512
    tq = min(256, seq)
    rs = tq
    lc = min(256, seq)
    return tm, tf, tq, rs, lc


def kernel(x_prompt, x_sample, cache_k, cache_v, state_mlstm_c, state_mlstm_n, state_mlstm_m,
           state_conv, state_ffn_conv, page_table,
           g_mix, w_in, b_if, att_lambda, att_subln, mlstm_norm, conv_w, w_out,
           g_ffn, w_up, ffn_conv_w, w_down, g_final):
    depth = w_in.shape[0]
    n_seq, seq, _ = x_prompt.shape
    bd = x_sample.shape[0]
    n_pool = cache_k.shape[1]
    tm, tf, tq, rs, lc = _tiles(seq)
    tps = seq // tm
    n_pages = page_table.shape[1]
    pages_step = math.gcd(32, n_pages)

    n_gate = 2 * N_HEADS_ML
    w_bf = w_in.astype(BF16)
    w_conv = w_bf[:, :, N_QKV_MLSTM + n_gate:]
    w_out_b = w_out.astype(BF16)
    w_up_b = w_up.astype(BF16)
    w_down_b = w_down.astype(BF16)
    b_if_pad = jnp.pad(b_if, ((0, 0), (0, GATE_W - n_gate))).reshape(depth, 1, GATE_W)

    slopes = jnp.asarray(_alibi_slopes(N_HEADS_ATT))
    cache_kt = cache_k.transpose(0, 1, 3, 2, 4)
    cache_vt = cache_v.transpose(0, 1, 3, 2, 4)

    c0p = jnp.zeros((n_seq, N_HEADS_ML, ML_DH, ML_DH), F32)
    n0p = jnp.zeros((n_seq, N_HEADS_ML, ML_DH), F32)
    m0p = jnp.full((n_seq, N_HEADS_ML, 128), -jnp.inf, F32)

    xp = x_prompt.reshape(n_seq * seq, D_MODEL)
    xs = x_sample.reshape(bd, D_MODEL)
    outs_p = [[] for _ in range(7)]
    outs_s = [[] for _ in range(7)]
    k_all = v_all = None
    gfin = g_final.reshape(1, D_MODEL)
    for l in range(depth):
        last = l == depth - 1
        lam_init = 0.8 - 0.6 * math.exp(-0.3 * l)
        gm = g_mix[l].reshape(1, D_MODEL)
        gf = g_ffn[l].reshape(1, D_MODEL)
        sub = att_subln[l].reshape(1, ATT_HW)
        mnorm = mlstm_norm[l].reshape(1, ML_W)

        proj, gates, q, k_all, v_all = _inproj(xp, gm, w_bf[l], w_conv[l], n_seq, seq, tm, l, depth,
                                               k_all, v_all)
        a = _attn_prompt(q, k_all, v_all, l, slopes, att_lambda[l], sub, n_seq, seq, tq, rs, lam_init)
        hm, c1, n1, m1 = _mlstm_prompt(proj, gates, b_if_pad[l], mnorm, c0p, n0p, m0p, n_seq, seq, lc)
        xp, cst = _outproj_prompt(xp, a, hm, proj, conv_w[l], w_out_b[l], n_seq, seq, tm)
        xp, stg, stv = _ffn_prompt(xp, gf, w_up_b[l], ffn_conv_w[l], w_down_b[l], gfin, last,
                                   n_seq, seq, tm, tf)
        outs_p[2].append(c1)
        outs_p[3].append(n1)
        outs_p[4].append(m1[:, :, 0])
        outs_p[5].append(cst[tps - 1::tps, 6:8, :])
        outs_p[6].append(jnp.concatenate([stg[tps - 1::tps, 6:8, :], stv[tps - 1::tps, 6:8, :]], axis=-1))

        proj_s, gates_s, q_s, k_s, v_s = _inproj(xs, gm, w_bf[l], w_conv[l], 1, bd, bd, 0, 1)
        q_s, k_s, v_s = q_s[0], k_s[0, 0], v_s[0, 0]
        scores = _attn_scores(q_s, cache_kt, page_table, l, pages_step)
        a_s = _attn_pv(scores, q_s, k_s, v_s, att_lambda[l], sub, cache_vt, page_table, l,
                       pages_step, lam_init)
        m0s = jnp.broadcast_to(state_mlstm_m[l][:, :, None], (bd, N_HEADS_ML, 128))
        hm_s, c1s, n1s, m1s = _mlstm_step(proj_s, gates_s, b_if_pad[l], mnorm, state_mlstm_c[l],
                                          state_mlstm_n[l], m0s)
        sc = state_conv[l]
        xs, u_s = _outproj_step(xs, a_s, hm_s, proj_s, conv_w[l], sc[:, 0], sc[:, 1], w_out_b[l])
        sf = state_ffn_conv[l]
        xs, ug_s, uv_s = _ffn_step(xs, gf, w_up_b[l], ffn_conv_w[l], sf[:, 0], sf[:, 1], w_down_b[l],
                                   gfin, last, tf)
        outs_s[0].append(k_s.transpose(1, 0, 2).reshape(bd, 1, N_HEADS_ATT, ATT_HW))
        outs_s[1].append(v_s.transpose(1, 0, 2).reshape(bd, 1, N_HEADS_ATT, ATT_HW))
        outs_s[2].append(c1s)
        outs_s[3].append(n1s)
        outs_s[4].append(m1s[:, :, 0])
        outs_s[5].append(jnp.stack([sc[:, 1], u_s], axis=1))
        outs_s[6].append(jnp.stack([sf[:, 1], jnp.concatenate([ug_s, uv_s], axis=-1)], axis=1))

    y_prompt = xp.reshape(n_seq, seq, D_MODEL)
    y_sample = xs.reshape(bd, 1, D_MODEL)
    c_p, n_p, m_p, conv_p, ffn_p = [jnp.stack(o, 0) for o in outs_p[2:]]
    k_p = k_all.transpose(0, 1, 3, 2, 4)
    v_p = v_all.transpose(0, 1, 3, 2, 4)
    k_s_, v_s_, c_s, n_s, m_s, conv_s, ffn_s = [jnp.stack(o, 0) for o in outs_s]
    return (y_prompt, y_sample, k_p, v_p, k_s_, v_s_, c_p, n_p, m_p, c_s, n_s, m_s,
            conv_p, conv_s, ffn_p, ffn_s)
```

```python
import functools
import math

import numpy as np
import jax
import jax.numpy as jnp
from jax import lax
from jax.experimental import pallas as pl
from jax.experimental.pallas import tpu as pltpu

F32 = jnp.float32
BF16 = jnp.bfloat16

D_MODEL = 2048
N_HEADS_ATT = 6
ATT_DH = 64
ATT_HW = 2 * ATT_DH
ATT_W = N_HEADS_ATT * ATT_HW
N_HEADS_ML = 4
ML_DH = 192
ML_W = N_HEADS_ML * ML_DH
CONV_C = D_MODEL - ATT_W - ML_W
D_FF = 5632
PAGE = 128
EPS = 1e-6

GATE_W = 128

VMEM_LIMIT = 56 * 1024 * 1024


def _cparams(sem):
    return pltpu.CompilerParams(dimension_semantics=sem, vmem_limit_bytes=VMEM_LIMIT)


def _alibi_slopes(n_heads):
    def geometric(n):
        start = 2.0 ** (-8.0 / n)
        return [start ** (i + 1) for i in range(n)]
    closest = 2 ** int(math.floor(math.log2(n_heads)))
    slopes = geometric(closest)
    if closest < n_heads:
        slopes = slopes + geometric(2 * closest)[0::2][: n_heads - closest]
    return np.asarray(slopes, dtype=np.float32)


def _rms(x, g):
    return x * lax.rsqrt(jnp.mean(x * x, axis=-1, keepdims=True) + EPS) * g


def _log_sigmoid(x):
    return jnp.minimum(x, 0.0) - jnp.log1p(jnp.exp(-jnp.abs(x)))


def _sigmoid(x):
    return 0.5 * (jnp.tanh(0.5 * x) + 1.0)


def _lambda(al_ref, lam_init):
    al = al_ref[...]
    s01 = jnp.sum(al[0:1] * al[1:2], axis=-1, keepdims=True)
    s23 = jnp.sum(al[2:3] * al[3:4], axis=-1, keepdims=True)
    return jnp.exp(s01) - jnp.exp(s23) + lam_init


def _dot_nt(a, b):
    return lax.dot_general(a, b, (((1,), (1,)), ((), ())), preferred_element_type=F32)


def _dot_tn(a, b):
    return lax.dot_general(a, b, (((0,), (0,)), ((), ())), preferred_element_type=F32)


def _dot(a, b):
    return jnp.dot(a, b, preferred_element_type=F32)


TN = ATT_W
N_QKV_MLSTM = 3 * ATT_W + 4 * ML_W
J_ML = 3
J_CONV = N_QKV_MLSTM // TN
N_REST = 4 * ML_W + 3 * CONV_C
R_MQ, R_MK, R_MV, R_MO = 0, ML_W, 2 * ML_W, 3 * ML_W
R_CB = 4 * ML_W
R_CC, R_CX = R_CB + CONV_C, R_CB + 2 * CONV_C


def _inproj_kernel(*refs, layer_slot):
    if layer_slot:
        refs = refs[2:]
    x_ref, g_ref, w_ref, wc_ref, wg_ref, rest_ref, og_ref, q_ref, k_ref, v_ref, h_ref = refs
    j = pl.program_id(2)

    @pl.when(j == 0)
    def _():
        h = _rms(x_ref[...], g_ref[...]).astype(BF16)
        h_ref[...] = h
        og_ref[...] = _dot_nt(h, wg_ref[...])

    for t, ref in enumerate((q_ref, k_ref, v_ref)):
        @pl.when(j == t)
        def _(ref=ref):
            res = _dot_nt(h_ref[...], w_ref[...])
            for h in range(N_HEADS_ATT):
                ref[h] = res[:, h * ATT_HW:(h + 1) * ATT_HW]

    @pl.when((j >= J_ML) & (j < J_CONV))
    def _():
        rest_ref[...] = _dot_nt(h_ref[...], w_ref[...])

    @pl.when(j >= J_CONV)
    def _():
        rest_ref[...] = _dot_nt(h_ref[...], wc_ref[...])


def _inproj(x, g, w_bf, w_conv, wl, n_seq, seq, tm, layer, depth, k_all=None, v_all=None):
    m = x.shape[0]
    tps = seq // tm
    nj = J_CONV + 3 * CONV_C // TN
    alias = k_all is not None
    rest_map = lambda b, i, j: (b * tps + i, jnp.maximum(j - J_ML, 0))
    row = lambda b, i, j: (b * tps + i, 0)
    const = lambda b, i, j: (0, 0)
    kv_spec = pl.BlockSpec((None, None, N_HEADS_ATT, tm, ATT_HW), lambda b, i, j: (layer, b, 0, i, 0))
    in_specs = [
        pl.BlockSpec((tm, D_MODEL), row),
        pl.BlockSpec((1, D_MODEL), const),
        pl.BlockSpec((None, TN, D_MODEL), lambda b, i, j: (wl, jnp.minimum(j, J_CONV - 1), 0)),
        pl.BlockSpec((None, TN, D_MODEL), lambda b, i, j: (wl, jnp.maximum(j - J_CONV, 0), 0)),
        pl.BlockSpec((None, GATE_W, D_MODEL), lambda b, i, j: (wl, N_QKV_MLSTM // GATE_W, 0)),
    ]
    args = [x, g, w_bf, w_conv, w_bf]
    if alias:
        in_specs = [pl.BlockSpec(memory_space=pl.ANY), pl.BlockSpec(memory_space=pl.ANY)] + in_specs
        args = [k_all, v_all] + args
    kv_shape = jax.ShapeDtypeStruct((depth, n_seq, N_HEADS_ATT, seq, ATT_HW), F32)
    return pl.pallas_call(
        functools.partial(_inproj_kernel, layer_slot=alias),
        grid=(n_seq, tps, nj),
        in_specs=in_specs,
        out_specs=[
            pl.BlockSpec((tm, TN), rest_map),
            pl.BlockSpec((tm, GATE_W), row),
            pl.BlockSpec((None, N_HEADS_ATT, tm, ATT_HW), lambda b, i, j: (b, 0, i, 0)),
            kv_spec, kv_spec,
        ],
        out_shape=[
            jax.ShapeDtypeStruct((m, N_REST), F32),
            jax.ShapeDtypeStruct((m, GATE_W), F32),
            jax.ShapeDtypeStruct((n_seq, N_HEADS_ATT, seq, ATT_HW), F32),
            kv_shape, kv_shape,
        ],
        scratch_shapes=[pltpu.VMEM((tm, D_MODEL), BF16)],
        input_output_aliases={0: 3, 1: 4} if alias else {},
        compiler_params=_cparams(("parallel", "parallel", "arbitrary")),
        name="inproj",
    )(*args)


def _attn_prompt_kernel(slopes_ref, qa_ref, qb_ref, k_ref, v_ref, al_ref, sub_ref, oa_ref, ob_ref,
                        q2_ref, kb_ref, vb_ref, m_ref, l_ref, acc_ref, *, tq, rs, nq, lam_init):
    h = pl.program_id(1)
    i = pl.program_id(2)
    slope = slopes_ref[h]
    tile_a, tile_b = i, nq - 1 - i

    @pl.when(i == 0)
    def _():
        kb_ref[...] = k_ref[...].astype(BF16)
        vb_ref[...] = v_ref[...].astype(BF16)

    lane = lax.broadcasted_iota(jnp.int32, (tq, ATT_HW), 1)
    for n, ref in enumerate((qa_ref, qb_ref)):
        q = ref[...] * (ATT_DH ** -0.5)
        q2_ref[2 * n * tq:(2 * n + 1) * tq, :] = jnp.where(lane < ATT_DH, q, 0.0).astype(BF16)
        q2_ref[(2 * n + 1) * tq:(2 * n + 2) * tq, :] = jnp.where(lane >= ATT_DH, q, 0.0).astype(BF16)
    m_ref[...] = jnp.full_like(m_ref, -jnp.inf)
    l_ref[...] = jnp.zeros_like(l_ref)
    acc_ref[...] = jnp.zeros_like(acc_ref)
    col = lax.broadcasted_iota(jnp.int32, (1, tq), 1)

    def block(row_base, tile, kblk, diag):
        start = pl.multiple_of(kblk * tq, tq)
        bias = slope * (col + (kblk - tile) * tq).astype(F32)
        for t in range(2 * tq // rs):
            r0 = row_base + t * rs
            rows = pl.ds(r0 if isinstance(r0, int) else pl.multiple_of(r0, rs), rs)
            q0 = (t * rs) % tq
            nk = min(tq, -(-(q0 + rs) // ATT_HW) * ATT_HW) if diag else tq
            k = kb_ref[pl.ds(start, nk), :]
            v = vb_ref[pl.ds(start, nk), :]
            s = _dot_nt(q2_ref[rows, :], k) + bias[:, :nk]
            if diag:
                r = lax.broadcasted_iota(jnp.int32, (rs, nk), 0) + q0
                c = lax.broadcasted_iota(jnp.int32, (rs, nk), 1)
                s = jnp.where(c <= r, s, -jnp.inf)
            m_old = m_ref[rows, :]
            m_new = jnp.maximum(m_old, jnp.max(s, axis=-1, keepdims=True))
            alpha = jnp.exp(m_old - m_new)
            p = jnp.exp(s - jnp.tile(m_new, (1, nk // ATT_HW)))
            l_ref[rows, :] = alpha * l_ref[rows, :] + jnp.sum(p, axis=-1, keepdims=True)
            acc_ref[rows, :] = alpha * acc_ref[rows, :] + _dot(p.astype(BF16), v)
            m_ref[rows, :] = m_new

    for u in range(nq - 1):
        to_b = (u >= i).astype(jnp.int32)
        block(to_b * (2 * tq), jnp.where(u >= i, tile_b, tile_a), u - to_b * i, False)
    block(0, tile_a, tile_a, True)
    block(2 * tq, tile_b, tile_b, True)

    lam = _lambda(al_ref, lam_init)
    o = acc_ref[...] / l_ref[...]
    for n, ref in enumerate((oa_ref, ob_ref)):
        on = o[2 * n * tq:(2 * n + 1) * tq] - lam * o[(2 * n + 1) * tq:(2 * n + 2) * tq]
        ref[...] = (_rms(on, sub_ref[...]) * (1.0 - lam_init)).astype(ref.dtype)


def _attn_prompt(q, k_all, v_all, layer, slopes, att_lambda, subln, n_seq, seq, tq, rs, lam_init):
    nq = seq // tq
    half = nq // 2
    w = ATT_HW
    kv_spec = pl.BlockSpec((None, None, None, seq, w), lambda b, h, i: (layer, b, h, 0, 0))
    o_shape = jax.ShapeDtypeStruct((n_seq, half, tq, ATT_W), BF16)
    lo, hi = pl.pallas_call(
        functools.partial(_attn_prompt_kernel, tq=tq, rs=rs, nq=nq, lam_init=lam_init),
        grid=(n_seq, N_HEADS_ATT, half),
        in_specs=[
            pl.BlockSpec(memory_space=pltpu.SMEM),
            pl.BlockSpec((None, None, tq, w), lambda b, h, i: (b, h, i, 0)),
            pl.BlockSpec((None, None, tq, w), lambda b, h, i: (b, h, nq - 1 - i, 0)),
            kv_spec, kv_spec,
            pl.BlockSpec((4, ATT_DH), lambda b, h, i: (0, 0)),
            pl.BlockSpec((1, w), lambda b, h, i: (0, 0)),
        ],
        out_specs=[pl.BlockSpec((None, None, tq, w), lambda b, h, i: (b, i, 0, h)),
                   pl.BlockSpec((None, None, tq, w), lambda b, h, i: (b, half - 1 - i, 0, h))],
        out_shape=[o_shape, o_shape],
        scratch_shapes=[pltpu.VMEM((4 * tq, w), BF16), pltpu.VMEM((seq, w), BF16),
                        pltpu.VMEM((seq, w), BF16), pltpu.VMEM((4 * tq, w), F32),
                        pltpu.VMEM((4 * tq, w), F32), pltpu.VMEM((4 * tq, w), F32)],
        compiler_params=_cparams(("parallel", "parallel", "arbitrary")),
        name="attn_prompt",
    )(slopes, q, q, k_all, v_all, att_lambda, subln)
    return jnp.concatenate([lo, hi], axis=1).reshape(n_seq * seq, ATT_W)


def _q_two_maps(q_row):
    r = lax.broadcasted_iota(jnp.int32, (8, ATT_HW), 0)
    c = lax.broadcasted_iota(jnp.int32, (8, ATT_HW), 1)
    keep = ((r == 0) & (c < ATT_DH)) | ((r == 1) & (c >= ATT_DH))
    return jnp.where(keep, q_row * (ATT_DH ** -0.5), 0.0)


def _scores_kernel(pt_ref, q_ref, *refs, n_pages_step):
    k_refs = refs[:n_pages_step]
    s_ref = refs[n_pages_step]
    b = pl.program_id(0)
    for h in range(N_HEADS_ATT):
        q2 = _q_two_maps(q_ref[h, pl.ds(b, 1), :]).astype(BF16)
        for i in range(n_pages_step):
            s_ref[h, :, i * PAGE:(i + 1) * PAGE] = _dot_nt(q2, k_refs[i][h].astype(BF16))


def _page_specs(layer, n_pages_step):
    def pmap(i):
        return lambda b, c, pt: (layer, pt[b, c * n_pages_step + i], 0, 0, 0)
    return [pl.BlockSpec((None, None, N_HEADS_ATT, PAGE, ATT_HW), pmap(i)) for i in range(n_pages_step)]


def _attn_scores(q, cache_kt, page_table, layer, n_pages_step):
    bd, n_pages = page_table.shape
    steps = n_pages // n_pages_step
    grid_spec = pltpu.PrefetchScalarGridSpec(
        num_scalar_prefetch=1,
        grid=(bd, steps),
        in_specs=[pl.BlockSpec((N_HEADS_ATT, bd, ATT_HW), lambda b, c, pt: (0, 0, 0))]
        + _page_specs(layer, n_pages_step),
        out_specs=pl.BlockSpec((None, N_HEADS_ATT, 8, n_pages_step * PAGE),
                               lambda b, c, pt: (b, 0, 0, c)),
    )
    return pl.pallas_call(
        functools.partial(_scores_kernel, n_pages_step=n_pages_step),
        grid_spec=grid_spec,
        out_shape=jax.ShapeDtypeStruct((bd, N_HEADS_ATT, 8, n_pages * PAGE), F32),
        compiler_params=_cparams(("parallel", "arbitrary")),
        name="attn_scores",
    )(page_table, q, *([cache_kt] * n_pages_step))


def _attn_pv_kernel(pt_ref, s_ref, q_ref, kn_ref, vn_ref, al_ref, sub_ref, *refs,
                    n_pages_step, past, lam_init, slopes):
    v_refs = refs[:n_pages_step]
    o_ref = refs[n_pages_step]
    a_ref, anew_ref, acc_ref = refs[n_pages_step + 1:]
    b = pl.program_id(0)
    c = pl.program_id(1)

    @pl.when(c == 0)
    def _():
        lam = _lambda(al_ref, lam_init)
        dist = (past - lax.broadcasted_iota(jnp.int32, (1, past), 1)).astype(F32)
        for h in range(N_HEADS_ATT):
            s = s_ref[h] - slopes[h] * dist
            qk = _q_two_maps(q_ref[h, pl.ds(b, 1), :]) * kn_ref[h, pl.ds(b, 1), :]
            s_new = jnp.sum(qk, axis=-1, keepdims=True)
            m = jnp.maximum(jnp.max(s, axis=-1, keepdims=True), s_new)
            p = jnp.exp(s - m)
            p_new = jnp.exp(s_new - m)
            l = jnp.sum(p, axis=-1, keepdims=True) + p_new
            p = p / l
            p_new = p_new / l
            a_ref[h] = jnp.broadcast_to(p[0:1] - lam * p[1:2], (8, past))
            anew_ref[h] = jnp.broadcast_to(p_new[0:1] - lam * p_new[1:2], (8, ATT_HW))
        acc_ref[...] = jnp.zeros_like(acc_ref)

    for h in range(N_HEADS_ATT):
        acc = acc_ref[h]
        for i in range(n_pages_step):
            start = pl.multiple_of((c * n_pages_step + i) * PAGE, PAGE)
            a = a_ref[h, :, pl.ds(start, PAGE)].astype(BF16)
            acc = acc + _dot(a, v_refs[i][h].astype(BF16))
        acc_ref[h] = acc

    @pl.when(c == pl.num_programs(1) - 1)
    def _():
        for h in range(N_HEADS_ATT):
            o = acc_ref[h][0:1] + anew_ref[h][0:1] * vn_ref[h, pl.ds(b, 1), :]
            o_ref[:, h * ATT_HW:(h + 1) * ATT_HW] = _rms(o, sub_ref[...]) * (1.0 - lam_init)


def _attn_pv(scores, q, k_new, v_new, att_lambda, subln, cache_vt, page_table, layer,
             n_pages_step, lam_init):
    bd, n_pages = page_table.shape
    past = n_pages * PAGE
    steps = n_pages // n_pages_step
    const2 = lambda b, c, pt: (0, 0)
    const3 = lambda b, c, pt: (0, 0, 0)
    hm = pl.BlockSpec((N_HEADS_ATT, bd, ATT_HW), const3)
    grid_spec = pltpu.PrefetchScalarGridSpec(
        num_scalar_prefetch=1,
        grid=(bd, steps),
        in_specs=[
            pl.BlockSpec((None, N_HEADS_ATT, 8, past), lambda b, c, pt: (b, 0, 0, 0)),
            hm, hm, hm,
            pl.BlockSpec((4, ATT_DH), const2),
            pl.BlockSpec((1, ATT_HW), const2),
        ] + _page_specs(layer, n_pages_step),
        out_specs=pl.BlockSpec((None, 1, ATT_W), lambda b, c, pt: (b, 0, 0)),
        scratch_shapes=[pltpu.VMEM((N_HEADS_ATT, 8, past), F32),
                        pltpu.VMEM((N_HEADS_ATT, 8, ATT_HW), F32),
                        pltpu.VMEM((N_HEADS_ATT, 8, ATT_HW), F32)],
    )
    out = pl.pallas_call(
        functools.partial(_attn_pv_kernel, n_pages_step=n_pages_step, past=past,
                          lam_init=lam_init, slopes=[float(x) for x in _alibi_slopes(N_HEADS_ATT)]),
        grid_spec=grid_spec,
        out_shape=jax.ShapeDtypeStruct((bd, 1, ATT_W), F32),
        compiler_params=_cparams(("parallel", "arbitrary")),
        name="attn_pv",
    )(page_table, scores, q, k_new, v_new, att_lambda, subln, *([cache_vt] * n_pages_step))
    return out.reshape(bd, ATT_W)


def _mlstm_chunk_kernel(q_ref, k_ref, v_ref, o_ref, g_ref, bias_ref, norm_ref, c0_ref, n0_ref,
                        m0_ref, h_ref, c_ref, n_ref, m_ref, *, L):
    @pl.when(pl.program_id(1) == 0)
    def _():
        c_ref[...] = c0_ref[...]
        n_ref[...] = n0_ref[...]
        m_ref[...] = m0_ref[...]

    g = g_ref[...] + bias_ref[...]
    g_t = g.T
    r = lax.broadcasted_iota(jnp.int32, (L, L), 0)
    c = lax.broadcasted_iota(jnp.int32, (L, L), 1)
    causal = c <= r

    def lanes(x, n):
        reps = [x] * (n // 128) + ([x[:, :n % 128]] if n % 128 else [])
        return reps[0] if len(reps) == 1 else jnp.concatenate(reps, axis=1)

    for h in range(N_HEADS_ML):
        sl = slice(h * ML_DH, (h + 1) * ML_DH)
        ig_r = g_t[h:h + 1, :]
        lf_r = _log_sigmoid(g_t[N_HEADS_ML + h:N_HEADS_ML + h + 1, :])
        ig_c = jnp.broadcast_to(g[:, h:h + 1], (L, 128))
        lf_c = jnp.broadcast_to(_log_sigmoid(g[:, N_HEADS_ML + h:N_HEADS_ML + h + 1]), (L, 128))
        b_c = jnp.broadcast_to(jnp.sum(jnp.where(causal, lf_r, 0.0), axis=1, keepdims=True),
                               (L, 128))
        b_r = jnp.sum(jnp.where(r <= c, lanes(lf_c, L), 0.0), axis=0, keepdims=True)
        m_prev = m_ref[h:h + 1, :]
        d = jnp.where(causal, lanes(b_c, L) + (ig_r - b_r), -jnp.inf)
        inter = b_c + m_prev
        m_t = jnp.maximum(inter, jnp.max(d, axis=1, keepdims=True))
        w_intra = jnp.exp(d - lanes(m_t, L))
        w_state = jnp.exp(inter - m_t)

        q = q_ref[:, sl]
        k = k_ref[:, sl] * (ML_DH ** -0.5)
        v = v_ref[:, sl]
        qb, kb, vb = q.astype(BF16), k.astype(BF16), v.astype(BF16)
        cst = c_ref[h]
        nst = n_ref[h:h + 1, :]
        a = w_intra * _dot_nt(qb, kb)
        num = _dot(a.astype(BF16), vb) + lanes(w_state, ML_DH) * _dot_nt(qb, cst.astype(BF16))
        den = jnp.sum(a, axis=1, keepdims=True) + w_state * jnp.sum(q * nst, axis=1, keepdims=True)
        hh = num * lanes(1.0 / jnp.maximum(jnp.abs(den), jnp.exp(-m_t)), ML_DH)

        m_new = m_t[L - 1:L, :]
        b_last = b_c[L - 1:L, :]
        w_s = lanes(jnp.exp(b_last - b_c + ig_c - m_new), ML_DH)
        decay = lanes(jnp.exp(b_last + m_prev - m_new), ML_DH)
        c_ref[h] = decay * cst + _dot_tn((w_s * v).astype(BF16), kb)
        n_ref[h:h + 1, :] = decay * nst + jnp.sum(w_s * k, axis=0, keepdims=True)
        m_ref[h:h + 1, :] = m_new

        hn = _rms(hh, norm_ref[:, sl])
        h_ref[:, sl] = (_sigmoid(o_ref[:, sl]) * hn).astype(h_ref.dtype)


def _mlstm_prompt(proj, gates, b_if_pad, norm, c0, n0, m0, n_seq, seq, L):
    nc = seq // L
    blk = lambda off: pl.BlockSpec((L, ML_W), lambda b, c, off=off: (b * nc + c, off // ML_W))
    const2 = lambda b, c: (0, 0)
    return pl.pallas_call(
        functools.partial(_mlstm_chunk_kernel, L=L),
        grid=(n_seq, nc),
        in_specs=[
            blk(R_MQ), blk(R_MK), blk(R_MV), blk(R_MO),
            pl.BlockSpec((L, GATE_W), lambda b, c: (b * nc + c, 0)),
            pl.BlockSpec((1, GATE_W), const2),
            pl.BlockSpec((1, ML_W), const2),
            pl.BlockSpec((None, N_HEADS_ML, ML_DH, ML_DH), lambda b, c: (b, 0, 0, 0)),
            pl.BlockSpec((None, N_HEADS_ML, ML_DH), lambda b, c: (b, 0, 0)),
            pl.BlockSpec((None, N_HEADS_ML, 128), lambda b, c: (b, 0, 0)),
        ],
        out_specs=[
            pl.BlockSpec((L, ML_W), lambda b, c: (b * nc + c, 0)),
            pl.BlockSpec((None, N_HEADS_ML, ML_DH, ML_DH), lambda b, c: (b, 0, 0, 0)),
            pl.BlockSpec((None, N_HEADS_ML, ML_DH), lambda b, c: (b, 0, 0)),
            pl.BlockSpec((None, N_HEADS_ML, 128), lambda b, c: (b, 0, 0)),
        ],
        out_shape=[
            jax.ShapeDtypeStruct((n_seq * seq, ML_W), BF16),
            jax.ShapeDtypeStruct((n_seq, N_HEADS_ML, ML_DH, ML_DH), F32),
            jax.ShapeDtypeStruct((n_seq, N_HEADS_ML, ML_DH), F32),
            jax.ShapeDtypeStruct((n_seq, N_HEADS_ML, 128), F32),
        ],
        compiler_params=_cparams(("parallel", "arbitrary")),
        name="mlstm_prompt",
    )(proj, proj, proj, proj, gates, b_if_pad, norm, c0, n0, m0)


def _mlstm_step_kernel(q_ref, k_ref, v_ref, o_ref, g_ref, bias_ref, norm_ref, c0_ref, n0_ref,
                       m0_ref, h_ref, c_ref, n_ref, m_ref):
    b = pl.program_id(0)
    g = g_ref[pl.ds(b, 1), :] + bias_ref[...]
    r = lax.broadcasted_iota(jnp.int32, (ML_DH, ML_DH), 0)
    c = lax.broadcasted_iota(jnp.int32, (ML_DH, ML_DH), 1)
    eye = r == c
    for h in range(N_HEADS_ML):
        sl = slice(h * ML_DH, (h + 1) * ML_DH)
        ig = g[:, h:h + 1]
        lf = _log_sigmoid(g[:, N_HEADS_ML + h:N_HEADS_ML + h + 1])
        m_prev = m0_ref[h:h + 1, 0:1]
        inter = lf + m_prev
        m_t = jnp.maximum(inter, ig)
        w_i = jnp.exp(ig - m_t)
        w_state = jnp.exp(inter - m_t)

        q = q_ref[pl.ds(b, 1), sl]
        k = k_ref[pl.ds(b, 1), sl] * (ML_DH ** -0.5)
        v = v_ref[pl.ds(b, 1), sl]
        cst = c0_ref[h]
        nst = n0_ref[h:h + 1, :]
        a = w_i * jnp.sum(q * k, axis=1, keepdims=True)
        cq = _dot_nt(jnp.broadcast_to(q, (8, ML_DH)).astype(BF16), cst.astype(BF16))[0:1]
        num = a * v + w_state * cq
        den = a + w_state * jnp.sum(q * nst, axis=1, keepdims=True)
        hh = num / jnp.maximum(jnp.abs(den), jnp.exp(-m_t))

        v_col = jnp.sum(jnp.where(eye, w_i * v, 0.0), axis=1, keepdims=True)
        c_ref[h] = w_state * cst + v_col * k
        n_ref[h:h + 1, :] = w_state * nst + w_i * k
        m_ref[h:h + 1, :] = jnp.broadcast_to(m_t, (1, 128))

        hn = _rms(hh, norm_ref[:, sl])
        h_ref[:, sl] = _sigmoid(o_ref[pl.ds(b, 1), sl]) * hn


def _mlstm_step(proj, gates, b_if_pad, norm, c0, n0, m0):
    bd = proj.shape[0]
    blk = lambda off: pl.BlockSpec((bd, ML_W), lambda b, off=off: (0, off // ML_W))
    const2 = lambda b: (0, 0)
    st4 = pl.BlockSpec((None, N_HEADS_ML, ML_DH, ML_DH), lambda b: (b, 0, 0, 0))
    st3 = pl.BlockSpec((None, N_HEADS_ML, ML_DH), lambda b: (b, 0, 0))
    stm = pl.BlockSpec((None, N_HEADS_ML, 128), lambda b: (b, 0, 0))
    out = pl.pallas_call(
        _mlstm_step_kernel,
        grid=(bd,),
        in_specs=[blk(R_MQ), blk(R_MK), blk(R_MV), blk(R_MO),
                  pl.BlockSpec((bd, GATE_W), const2),
                  pl.BlockSpec((1, GATE_W), const2),
                  pl.BlockSpec((1, ML_W), const2),
                  st4, st3, stm],
        out_specs=[pl.BlockSpec((None, 1, ML_W), lambda b: (b, 0, 0)), st4, st3, stm],
        out_shape=[
            jax.ShapeDtypeStruct((bd, 1, ML_W), F32),
            jax.ShapeDtypeStruct((bd, N_HEADS_ML, ML_DH, ML_DH), F32),
            jax.ShapeDtypeStruct((bd, N_HEADS_ML, ML_DH), F32),
            jax.ShapeDtypeStruct((bd, N_HEADS_ML, 128), F32),
        ],
        compiler_params=_cparams(("parallel",)),
        name="mlstm_step",
    )(proj, proj, proj, proj, gates, b_if_pad, norm, c0, n0, m0)
    return (out[0].reshape(bd, ML_W),) + tuple(out[1:])


def _conv3_rows(u, w_ref, prev8):
    w0, w1, w2 = w_ref[0:1, :], w_ref[1:2, :], w_ref[2:3, :]
    y = w0 * pltpu.roll(u, 2, 0) + w1 * pltpu.roll(u, 1, 0) + w2 * u
    head = u[0:8, :]
    row = lax.broadcasted_iota(jnp.int32, head.shape, 0)
    p1 = prev8[7:8, :]
    p2 = prev8[6:7, :]
    h1 = jnp.where(row == 0, p1, pltpu.roll(head, 1, 0))
    h2 = jnp.where(row == 0, p2, jnp.where(row == 1, p1, pltpu.roll(head, 2, 0)))
    return jnp.concatenate([w0 * h2 + w1 * h1 + w2 * head, y[8:, :]], axis=0)


def _conv3_state(u, w_ref, s0, s1):
    return w_ref[0:1, :] * s0 + w_ref[1:2, :] * s1 + w_ref[2:3, :] * u


def _outproj_prompt_kernel(x_ref, a_ref, hm_ref, cb_ref, cc_ref, cx_ref, cw_ref, w_ref,
                           o_ref, st_ref, carry_ref, *, tiles_per_seq):
    i = pl.program_id(0)
    u = cc_ref[...] * cx_ref[...]
    prev8 = jnp.where(i % tiles_per_seq == 0, 0.0, carry_ref[...])
    yc = cb_ref[...] * _conv3_rows(u, cw_ref, prev8)
    tail = u[u.shape[0] - 8:, :]
    carry_ref[...] = tail
    st_ref[...] = tail
    acc = _dot(a_ref[...], w_ref[0:ATT_W, :])
    acc = acc + _dot(hm_ref[...], w_ref[ATT_W:ATT_W + ML_W, :])
    acc = acc + _dot(yc.astype(BF16), w_ref[ATT_W + ML_W:, :])
    o_ref[...] = x_ref[...] + acc


def _outproj_prompt(x, a, hm, proj, conv_w, w_out, wl, n_seq, seq, tm):
    m = x.shape[0]
    tps = seq // tm
    cblk = lambda off: pl.BlockSpec((tm, CONV_C), lambda i, off=off: (i, off // CONV_C))
    return pl.pallas_call(
        functools.partial(_outproj_prompt_kernel, tiles_per_seq=tps),
        grid=(m // tm,),
        in_specs=[
            pl.BlockSpec((tm, D_MODEL), lambda i: (i, 0)),
            pl.BlockSpec((tm, ATT_W), lambda i: (i, 0)),
            pl.BlockSpec((tm, ML_W), lambda i: (i, 0)),
            cblk(R_CB), cblk(R_CC), cblk(R_CX),
            pl.BlockSpec((3, CONV_C), lambda i: (0, 0)),
            pl.BlockSpec((None, D_MODEL, D_MODEL), lambda i: (wl, 0, 0)),
        ],
        out_specs=[
            pl.BlockSpec((tm, D_MODEL), lambda i: (i, 0)),
            pl.BlockSpec((None, 8, CONV_C), lambda i: (i, 0, 0)),
        ],
        out_shape=[
            jax.ShapeDtypeStruct((m, D_MODEL), F32),
            jax.ShapeDtypeStruct((m // tm, 8, CONV_C), F32),
        ],
        scratch_shapes=[pltpu.VMEM((8, CONV_C), F32)],
        compiler_params=_cparams(("arbitrary",)),
        name="outproj_prompt",
    )(x, a, hm, proj, proj, proj, conv_w, w_out)


def _outproj_step_kernel(x_ref, a_ref, hm_ref, cb_ref, cc_ref, cx_ref, cw_ref, s0_ref, s1_ref,
                         w_ref, o_ref, u_ref):
    u = cc_ref[...] * cx_ref[...]
    u_ref[...] = u
    yc = cb_ref[...] * _conv3_state(u, cw_ref, s0_ref[...], s1_ref[...])
    acc = _dot(a_ref[...].astype(BF16), w_ref[0:ATT_W, :])
    acc = acc + _dot(hm_ref[...].astype(BF16), w_ref[ATT_W:ATT_W + ML_W, :])
    acc = acc + _dot(yc.astype(BF16), w_ref[ATT_W + ML_W:, :])
    o_ref[...] = x_ref[...] + acc


def _outproj_step(x, a, hm, proj, conv_w, s0, s1, w_out, wl):
    bd = x.shape[0]
    full = lambda shp: pl.BlockSpec(shp, lambda i: (0, 0))
    cblk = lambda off: pl.BlockSpec((bd, CONV_C), lambda i, off=off: (0, off // CONV_C))
    return pl.pallas_call(
        _outproj_step_kernel,
        grid=(1,),
        in_specs=[full((bd, D_MODEL)), full((bd, ATT_W)), full((bd, ML_W)),
                  cblk(R_CB), cblk(R_CC), cblk(R_CX),
                  full((3, CONV_C)), full((bd, CONV_C)), full((bd, CONV_C)),
                  pl.BlockSpec((None, D_MODEL, D_MODEL), lambda i: (wl, 0, 0))],
        out_specs=[full((bd, D_MODEL)), full((bd, CONV_C))],
        out_shape=[jax.ShapeDtypeStruct((bd, D_MODEL), F32),
                   jax.ShapeDtypeStruct((bd, CONV_C), F32)],
        compiler_params=_cparams(("arbitrary",)),
        name="outproj_step",
    )(x, a, hm, proj, proj, proj, conv_w, s0, s1, w_out)


def _ffn_prompt_kernel(x_ref, g_ref, wg_ref, wv_ref, cwg_ref, cwv_ref, wd_ref, gfin_ref,
                       o_ref, stg_ref, stv_ref, h_ref, carry_ref, *, tiles_per_seq, final_norm):
    i = pl.program_id(0)
    j = pl.program_id(1)

    @pl.when(j == 0)
    def _():
        x = x_ref[...]
        h_ref[...] = _rms(x, g_ref[...]).astype(BF16)
        o_ref[...] = x

    h = h_ref[...]
    ug = _dot(h, wg_ref[...])
    uv = _dot(h, wv_ref[...])
    start = i % tiles_per_seq == 0
    pg = jnp.where(start, 0.0, carry_ref[j, 0])
    pv = jnp.where(start, 0.0, carry_ref[j, 1])
    cg = _conv3_rows(ug, cwg_ref, pg)
    cv = _conv3_rows(uv, cwv_ref, pv)
    tm = ug.shape[0]
    tg = ug[tm - 8:, :]
    tv = uv[tm - 8:, :]
    carry_ref[j, 0] = tg
    carry_ref[j, 1] = tv
    stg_ref[...] = tg
    stv_ref[...] = tv
    act = (cg * _sigmoid(cg) * cv).astype(BF16)
    o_ref[...] += _dot(act, wd_ref[...])
    if final_norm:
        @pl.when(j == pl.num_programs(1) - 1)
        def _():
            o_ref[...] = _rms(o_ref[...], gfin_ref[...])


def _ffn_prompt(x, g, w_up, ffn_conv_w, w_down, wl, g_final, final_norm, n_seq, seq, tm, tf):
    m = x.shape[0]
    tps = seq // tm
    nj = D_FF // tf
    return pl.pallas_call(
        functools.partial(_ffn_prompt_kernel, tiles_per_seq=tps, final_norm=final_norm),
        grid=(m // tm, nj),
        in_specs=[
            pl.BlockSpec((tm, D_MODEL), lambda i, j: (i, 0)),
            pl.BlockSpec((1, D_MODEL), lambda i, j: (0, 0)),
            pl.BlockSpec((None, D_MODEL, tf), lambda i, j: (wl, 0, j)),
            pl.BlockSpec((None, D_MODEL, tf), lambda i, j: (wl, 0, nj + j)),
            pl.BlockSpec((3, tf), lambda i, j: (0, j)),
            pl.BlockSpec((3, tf), lambda i, j: (0, nj + j)),
            pl.BlockSpec((None, tf, D_MODEL), lambda i, j: (wl, j, 0)),
            pl.BlockSpec((1, D_MODEL), lambda i, j: (0, 0)),
        ],
        out_specs=[
            pl.BlockSpec((tm, D_MODEL), lambda i, j: (i, 0)),
            pl.BlockSpec((None, 8, tf), lambda i, j: (i, 0, j)),
            pl.BlockSpec((None, 8, tf), lambda i, j: (i, 0, j)),
        ],
        out_shape=[
            jax.ShapeDtypeStruct((m, D_MODEL), F32),
            jax.ShapeDtypeStruct((m // tm, 8, D_FF), F32),
            jax.ShapeDtypeStruct((m // tm, 8, D_FF), F32),
        ],
        scratch_shapes=[pltpu.VMEM((tm, D_MODEL), BF16), pltpu.VMEM((nj, 2, 8, tf), F32)],
        compiler_params=_cparams(("arbitrary", "arbitrary")),
        name="ffn_prompt",
    )(x, g, w_up, w_up, ffn_conv_w, ffn_conv_w, w_down, g_final)


def _ffn_step_kernel(x_ref, g_ref, wg_ref, wv_ref, cwg_ref, cwv_ref, s0g_ref, s1g_ref, s0v_ref,
                     s1v_ref, wd_ref, gfin_ref, o_ref, ug_ref, uv_ref, h_ref, *, final_norm):
    @pl.when(pl.program_id(0) == 0)
    def _():
        x = x_ref[...]
        h_ref[...] = _rms(x, g_ref[...]).astype(BF16)
        o_ref[...] = x

    h = h_ref[...]
    ug = _dot(h, wg_ref[...])
    uv = _dot(h, wv_ref[...])
    ug_ref[...] = ug
    uv_ref[...] = uv
    cg = _conv3_state(ug, cwg_ref, s0g_ref[...], s1g_ref[...])
    cv = _conv3_state(uv, cwv_ref, s0v_ref[...], s1v_ref[...])
    act = (cg * _sigmoid(cg) * cv).astype(BF16)
    o_ref[...] += _dot(act, wd_ref[...])
    if final_norm:
        @pl.when(pl.program_id(0) == pl.num_programs(0) - 1)
        def _():
            o_ref[...] = _rms(o_ref[...], gfin_ref[...])


def _ffn_step(x, g, w_up, ffn_conv_w, s0, s1, w_down, wl, g_final, final_norm, tf):
    bd = x.shape[0]
    nj = D_FF // tf
    gcol = lambda rows: pl.BlockSpec((rows, tf), lambda j: (0, j))
    vcol = lambda rows: pl.BlockSpec((rows, tf), lambda j: (0, nj + j))
    return pl.pallas_call(
        functools.partial(_ffn_step_kernel, final_norm=final_norm),
        grid=(nj,),
        in_specs=[
            pl.BlockSpec((bd, D_MODEL), lambda j: (0, 0)),
            pl.BlockSpec((1, D_MODEL), lambda j: (0, 0)),
            pl.BlockSpec((None, D_MODEL, tf), lambda j: (wl, 0, j)),
            pl.BlockSpec((None, D_MODEL, tf), lambda j: (wl, 0, nj + j)),
            gcol(3), vcol(3),
            gcol(bd), gcol(bd), vcol(bd), vcol(bd),
            pl.BlockSpec((None, tf, D_MODEL), lambda j: (wl, j, 0)),
            pl.BlockSpec((1, D_MODEL), lambda j: (0, 0)),
        ],
        out_specs=[pl.BlockSpec((bd, D_MODEL), lambda j: (0, 0)), gcol(bd), gcol(bd)],
        out_shape=[jax.ShapeDtypeStruct((bd, D_MODEL), F32),
                   jax.ShapeDtypeStruct((bd, D_FF), F32),
                   jax.ShapeDtypeStruct((bd, D_FF), F32)],
        scratch_shapes=[pltpu.VMEM((bd, D_MODEL), BF16)],
        compiler_params=_cparams(("arbitrary",)),
        name="ffn_step",
    )(x, g, w_up, w_up, ffn_conv_w, ffn_conv_w, s0, s1, s0, s1, w_down, g_final)


def _tiles(seq):
    tm = min(512, seq)
    tf = 512
    tq = min(256, seq)
    rs = tq
    lc = min(256, seq)
    return tm, tf, tq, rs, lc


def kernel(x_prompt, x_sample, cache_k, cache_v, state_mlstm_c, state_mlstm_n, state_mlstm_m,
           state_conv, state_ffn_conv, page_table,
           g_mix, w_in, b_if, att_lambda, att_subln, mlstm_norm, conv_w, w_out,
           g_ffn, w_up, ffn_conv_w, w_down, g_final):
    depth = w_in.shape[0]
    n_seq, seq, _ = x_prompt.shape
    bd = x_sample.shape[0]
    n_pool = cache_k.shape[1]
    tm, tf, tq, rs, lc = _tiles(seq)
    tps = seq // tm
    n_pages = page_table.shape[1]
    pages_step = math.gcd(32, n_pages)

    n_gate = 2 * N_HEADS_ML
    w_bf = w_in.transpose(0, 2, 1).astype(BF16)
    w_conv = w_bf[:, N_QKV_MLSTM + n_gate:, :]
    w_out_b = w_out.astype(BF16)
    w_up_b = w_up.astype(BF16)
    w_down_b = w_down.astype(BF16)
    b_if_pad = jnp.pad(b_if, ((0, 0), (0, GATE_W - n_gate))).reshape(depth, 1, GATE_W)

    slopes = jnp.asarray(_alibi_slopes(N_HEADS_ATT))
    cache_kt = cache_k.transpose(0, 1, 3, 2, 4)
    cache_vt = cache_v.transpose(0, 1, 3, 2, 4)

    c0p = jnp.zeros((n_seq, N_HEADS_ML, ML_DH, ML_DH), F32)
    n0p = jnp.zeros((n_seq, N_HEADS_ML, ML_DH), F32)
    m0p = jnp.full((n_seq, N_HEADS_ML, 128), -jnp.inf, F32)

    xp = x_prompt.reshape(n_seq * seq, D_MODEL)
    xs = x_sample.reshape(bd, D_MODEL)
    outs_p = [[] for _ in range(7)]
    outs_s = [[] for _ in range(7)]
    k_all = v_all = None
    gfin = g_final.reshape(1, D_MODEL)
    for l in range(depth):
        last = l == depth - 1
        lam_init = 0.8 - 0.6 * math.exp(-0.3 * l)
        gm = g_mix[l].reshape(1, D_MODEL)
        gf = g_ffn[l].reshape(1, D_MODEL)
        sub = att_subln[l].reshape(1, ATT_HW)
        mnorm = mlstm_norm[l].reshape(1, ML_W)

        proj, gates, q, k_all, v_all = _inproj(xp, gm, w_bf, w_conv, l, n_seq, seq, tm, l, depth,
                                               k_all, v_all)
        a = _attn_prompt(q, k_all, v_all, l, slopes, att_lambda[l], sub, n_seq, seq, tq, rs, lam_init)
        hm, c1, n1, m1 = _mlstm_prompt(proj, gates, b_if_pad[l], mnorm, c0p, n0p, m0p, n_seq, seq, lc)
        xp, cst = _outproj_prompt(xp, a, hm, proj, conv_w[l], w_out_b, l, n_seq, seq, tm)
        xp, stg, stv = _ffn_prompt(xp, gf, w_up_b, ffn_conv_w[l], w_down_b, l, gfin, last,
                                   n_seq, seq, tm, tf)
        outs_p[2].append(c1)
        outs_p[3].append(n1)
        outs_p[4].append(m1[:, :, 0])
        outs_p[5].append(cst[tps - 1::tps, 6:8, :])
        outs_p[6].append(jnp.concatenate([stg[tps - 1::tps, 6:8, :], stv[tps - 1::tps, 6:8, :]], axis=-1))

        proj_s, gates_s, q_s, k_s, v_s = _inproj(xs, gm, w_bf, w_conv, l, 1, bd, bd, 0, 1)
        q_s, k_s, v_s = q_s[0], k_s[0, 0], v_s[0, 0]
        scores = _attn_scores(q_s, cache_kt, page_table, l, pages_step)
        a_s = _attn_pv(scores, q_s, k_s, v_s, att_lambda[l], sub, cache_vt, page_table, l,
                       pages_step, lam_init)
        m0s = jnp.broadcast_to(state_mlstm_m[l][:, :, None], (bd, N_HEADS_ML, 128))
        hm_s, c1s, n1s, m1s = _mlstm_step(proj_s, gates_s, b_if_pad[l], mnorm, state_mlstm_c[l],
                                          state_mlstm_n[l], m0s)
        sc = state_conv[l]
        xs, u_s = _outproj_step(xs, a_s, hm_s, proj_s, conv_w[l], sc[:, 0], sc[:, 1], w_out_b, l)
        sf = state_ffn_conv[l]
        xs, ug_s, uv_s = _ffn_step(xs, gf, w_up_b, ffn_conv_w[l], sf[:, 0], sf[:, 1], w_down_b, l,
                                   gfin, last, tf)
        outs_s[0].append(k_s.transpose(1, 0, 2).reshape(bd, 1, N_HEADS_ATT, ATT_HW))
        outs_s[1].append(v_s.transpose(1, 0, 2).reshape(bd, 1, N_HEADS_ATT, ATT_HW))
        outs_s[2].append(c1s)
        outs_s[3].append(n1s)
        outs_s[4].append(m1s[:, :, 0])
        outs_s[5].append(jnp.stack([sc[:, 1], u_s], axis=1))
        outs_s[6].append(jnp.stack([sf[:, 1], jnp.concatenate([ug_s, uv_s], axis=-1)], axis=1))

    y_prompt = xp.reshape(n_seq, seq, D_MODEL)
    y_sample = xs.reshape(bd, 1, D_MODEL)
    c_p, n_p, m_p, conv_p, ffn_p = [jnp.stack(o, 0) for o in outs_p[2:]]
    k_p = k_all.transpose(0, 1, 3, 2, 4)
    v_p = v_all.transpose(0, 1, 3, 2, 4)
    k_s_, v_s_, c_s, n_s, m_s, conv_s, ffn_s = [jnp.stack(o, 0) for o in outs_s]
    return (y_prompt, y_sample, k_p, v_p, k_s_, v_s_, c_p, n_p, m_p, c_s, n_s, m_s,
            conv_p, conv_s, ffn_p, ffn_s)
```

```python
import functools
import math

import numpy as np
import jax
import jax.numpy as jnp
from jax import lax
from jax.experimental import pallas as pl
from jax.experimental.pallas import tpu as pltpu

F32 = jnp.float32
BF16 = jnp.bfloat16

D_MODEL = 2048
N_HEADS_ATT = 6
ATT_DH = 64
ATT_HW = 2 * ATT_DH
ATT_W = N_HEADS_ATT * ATT_HW
N_HEADS_ML = 4
ML_DH = 192
ML_W = N_HEADS_ML * ML_DH
CONV_C = D_MODEL - ATT_W - ML_W
D_FF = 5632
PAGE = 128
EPS = 1e-6
LOG2E = 1.4426950408889634

GATE_W = 128

VMEM_LIMIT = 56 * 1024 * 1024


def _cparams(sem):
    return pltpu.CompilerParams(dimension_semantics=sem, vmem_limit_bytes=VMEM_LIMIT)


def _alibi_slopes(n_heads):
    def geometric(n):
        start = 2.0 ** (-8.0 / n)
        return [start ** (i + 1) for i in range(n)]
    closest = 2 ** int(math.floor(math.log2(n_heads)))
    slopes = geometric(closest)
    if closest < n_heads:
        slopes = slopes + geometric(2 * closest)[0::2][: n_heads - closest]
    return np.asarray(slopes, dtype=np.float32)


def _rms(x, g):
    return x * lax.rsqrt(jnp.mean(x * x, axis=-1, keepdims=True) + EPS) * g


def _log_sigmoid(x):
    return jnp.minimum(x, 0.0) - jnp.log1p(jnp.exp(-jnp.abs(x)))


def _sigmoid(x):
    return 0.5 * (jnp.tanh(0.5 * x) + 1.0)


def _lambda(al_ref, lam_init):
    al = al_ref[...]
    s01 = jnp.sum(al[0:1] * al[1:2], axis=-1, keepdims=True)
    s23 = jnp.sum(al[2:3] * al[3:4], axis=-1, keepdims=True)
    return jnp.exp(s01) - jnp.exp(s23) + lam_init


def _dot_nt(a, b):
    return lax.dot_general(a, b, (((1,), (1,)), ((), ())), preferred_element_type=F32)


def _dot_tn(a, b):
    return lax.dot_general(a, b, (((0,), (0,)), ((), ())), preferred_element_type=F32)


def _dot(a, b):
    return jnp.dot(a, b, preferred_element_type=F32)


TN = ATT_W
N_QKV_MLSTM = 3 * ATT_W + 4 * ML_W
J_ML = 3
J_CONV = N_QKV_MLSTM // TN
N_REST = 4 * ML_W + 3 * CONV_C
R_MQ, R_MK, R_MV, R_MO = 0, ML_W, 2 * ML_W, 3 * ML_W
R_CB = 4 * ML_W
R_CC, R_CX = R_CB + CONV_C, R_CB + 2 * CONV_C


def _inproj_kernel(*refs, layer_slot):
    if layer_slot:
        refs = refs[2:]
    x_ref, g_ref, w_ref, wc_ref, wg_ref, rest_ref, og_ref, q_ref, k_ref, v_ref, h_ref = refs
    j = pl.program_id(2)

    @pl.when(j == 0)
    def _():
        h = _rms(x_ref[...], g_ref[...]).astype(BF16)
        h_ref[...] = h
        og_ref[...] = _dot_nt(h, wg_ref[...])

    for t, ref in enumerate((q_ref, k_ref, v_ref)):
        @pl.when(j == t)
        def _(ref=ref):
            res = _dot_nt(h_ref[...], w_ref[...])
            for h in range(N_HEADS_ATT):
                ref[h] = res[:, h * ATT_HW:(h + 1) * ATT_HW]

    @pl.when((j >= J_ML) & (j < J_CONV))
    def _():
        rest_ref[...] = _dot_nt(h_ref[...], w_ref[...])

    @pl.when(j >= J_CONV)
    def _():
        rest_ref[...] = _dot_nt(h_ref[...], wc_ref[...])


def _inproj(x, g, w_bf, w_conv, wl, n_seq, seq, tm, layer, depth, k_all=None, v_all=None):
    m = x.shape[0]
    tps = seq // tm
    nj = J_CONV + 3 * CONV_C // TN
    alias = k_all is not None
    rest_map = lambda b, i, j: (b * tps + i, jnp.maximum(j - J_ML, 0))
    row = lambda b, i, j: (b * tps + i, 0)
    const = lambda b, i, j: (0, 0)
    kv_spec = pl.BlockSpec((None, None, N_HEADS_ATT, tm, ATT_HW), lambda b, i, j: (layer, b, 0, i, 0))
    in_specs = [
        pl.BlockSpec((tm, D_MODEL), row, pipeline_mode=pl.Buffered(1)),
        pl.BlockSpec((1, D_MODEL), const),
        pl.BlockSpec((None, TN, D_MODEL), lambda b, i, j: (wl, jnp.minimum(j, J_CONV - 1), 0)),
        pl.BlockSpec((None, TN, D_MODEL), lambda b, i, j: (wl, jnp.maximum(j - J_CONV, 0), 0)),
        pl.BlockSpec((None, GATE_W, D_MODEL), lambda b, i, j: (wl, N_QKV_MLSTM // GATE_W, 0)),
    ]
    args = [x, g, w_bf, w_conv, w_bf]
    if alias:
        in_specs = [pl.BlockSpec(memory_space=pl.ANY), pl.BlockSpec(memory_space=pl.ANY)] + in_specs
        args = [k_all, v_all] + args
    kv_shape = jax.ShapeDtypeStruct((depth, n_seq, N_HEADS_ATT, seq, ATT_HW), F32)
    return pl.pallas_call(
        functools.partial(_inproj_kernel, layer_slot=alias),
        grid=(n_seq, tps, nj),
        in_specs=in_specs,
        out_specs=[
            pl.BlockSpec((tm, TN), rest_map),
            pl.BlockSpec((tm, GATE_W), row),
            pl.BlockSpec((None, N_HEADS_ATT, tm, ATT_HW), lambda b, i, j: (b, 0, i, 0)),
            kv_spec, kv_spec,
        ],
        out_shape=[
            jax.ShapeDtypeStruct((m, N_REST), F32),
            jax.ShapeDtypeStruct((m, GATE_W), F32),
            jax.ShapeDtypeStruct((n_seq, N_HEADS_ATT, seq, ATT_HW), F32),
            kv_shape, kv_shape,
        ],
        scratch_shapes=[pltpu.VMEM((tm, D_MODEL), BF16)],
        input_output_aliases={0: 3, 1: 4} if alias else {},
        compiler_params=_cparams(("parallel", "parallel", "arbitrary")),
        name="inproj",
    )(*args)


def _attn_prompt_kernel(slopes_ref, qa_ref, qb_ref, k_ref, v_ref, al_ref, sub_ref, oa_ref, ob_ref,
                        q2_ref, kb_ref, vb_ref, m_ref, l_ref, acc_ref, *, tq, rs, nq, lam_init):
    h = pl.program_id(1)
    i = pl.program_id(2)
    slope = slopes_ref[h]
    tile_a, tile_b = i, nq - 1 - i

    @pl.when(i == 0)
    def _():
        kb_ref[...] = k_ref[...].astype(BF16)
        vb_ref[...] = v_ref[...].astype(BF16)

    lane = lax.broadcasted_iota(jnp.int32, (tq, ATT_HW), 1)
    for n, ref in enumerate((qa_ref, qb_ref)):
        q = ref[...] * (ATT_DH ** -0.5 * LOG2E)
        q2_ref[2 * n * tq:(2 * n + 1) * tq, :] = jnp.where(lane < ATT_DH, q, 0.0).astype(BF16)
        q2_ref[(2 * n + 1) * tq:(2 * n + 2) * tq, :] = jnp.where(lane >= ATT_DH, q, 0.0).astype(BF16)
    m_ref[...] = jnp.full_like(m_ref, -jnp.inf)
    l_ref[...] = jnp.zeros_like(l_ref)
    acc_ref[...] = jnp.zeros_like(acc_ref)
    col = lax.broadcasted_iota(jnp.int32, (1, tq), 1)

    def block(row_base, tile, kblk, diag):
        start = pl.multiple_of(kblk * tq, tq)
        bias = (slope * LOG2E) * (col + (kblk - tile) * tq).astype(F32)
        for t in range(2 * tq // rs):
            r0 = row_base + t * rs
            rows = pl.ds(r0 if isinstance(r0, int) else pl.multiple_of(r0, rs), rs)
            q0 = (t * rs) % tq
            nk = min(tq, -(-(q0 + rs) // ATT_HW) * ATT_HW) if diag else tq
            k = kb_ref[pl.ds(start, nk), :]
            v = vb_ref[pl.ds(start, nk), :]
            s = _dot_nt(q2_ref[rows, :], k) + bias[:, :nk]
            if diag:
                r = lax.broadcasted_iota(jnp.int32, (rs, nk), 0) + q0
                c = lax.broadcasted_iota(jnp.int32, (rs, nk), 1)
                s = jnp.where(c <= r, s, -jnp.inf)
            m_old = m_ref[rows, :]
            m_new = jnp.maximum(m_old, jnp.max(s, axis=-1, keepdims=True))
            alpha = jnp.exp2(m_old - m_new)
            p = jnp.exp2(s - jnp.tile(m_new, (1, nk // ATT_HW)))
            l_ref[rows, :] = alpha * l_ref[rows, :] + jnp.sum(p, axis=-1, keepdims=True)
            acc_ref[rows, :] = alpha * acc_ref[rows, :] + _dot(p.astype(BF16), v)
            m_ref[rows, :] = m_new

    for u in range(nq - 1):
        to_b = (u >= i).astype(jnp.int32)
        block(to_b * (2 * tq), jnp.where(u >= i, tile_b, tile_a), u - to_b * i, False)
    block(0, tile_a, tile_a, True)
    block(2 * tq, tile_b, tile_b, True)

    lam = _lambda(al_ref, lam_init)
    o = acc_ref[...] / l_ref[...]
    for n, ref in enumerate((oa_ref, ob_ref)):
        on = o[2 * n * tq:(2 * n + 1) * tq] - lam * o[(2 * n + 1) * tq:(2 * n + 2) * tq]
        ref[...] = (_rms(on, sub_ref[...]) * (1.0 - lam_init)).astype(ref.dtype)


def _attn_prompt(q, k_all, v_all, layer, slopes, att_lambda, subln, n_seq, seq, tq, rs, lam_init):
    nq = seq // tq
    half = nq // 2
    w = ATT_HW
    kv_spec = pl.BlockSpec((None, None, None, seq, w), lambda b, h, i: (layer, b, h, 0, 0))
    o_shape = jax.ShapeDtypeStruct((n_seq, half, tq, ATT_W), BF16)
    lo, hi = pl.pallas_call(
        functools.partial(_attn_prompt_kernel, tq=tq, rs=rs, nq=nq, lam_init=lam_init),
        grid=(n_seq, N_HEADS_ATT, half),
        in_specs=[
            pl.BlockSpec(memory_space=pltpu.SMEM),
            pl.BlockSpec((None, None, tq, w), lambda b, h, i: (b, h, i, 0)),
            pl.BlockSpec((None, None, tq, w), lambda b, h, i: (b, h, nq - 1 - i, 0)),
            kv_spec, kv_spec,
            pl.BlockSpec((4, ATT_DH), lambda b, h, i: (0, 0)),
            pl.BlockSpec((1, w), lambda b, h, i: (0, 0)),
        ],
        out_specs=[pl.BlockSpec((None, None, tq, w), lambda b, h, i: (b, i, 0, h)),
                   pl.BlockSpec((None, None, tq, w), lambda b, h, i: (b, half - 1 - i, 0, h))],
        out_shape=[o_shape, o_shape],
        scratch_shapes=[pltpu.VMEM((4 * tq, w), BF16), pltpu.VMEM((seq, w), BF16),
                        pltpu.VMEM((seq, w), BF16), pltpu.VMEM((4 * tq, w), F32),
                        pltpu.VMEM((4 * tq, w), F32), pltpu.VMEM((4 * tq, w), F32)],
        compiler_params=_cparams(("parallel", "parallel", "arbitrary")),
        name="attn_prompt",
    )(slopes, q, q, k_all, v_all, att_lambda, subln)
    return jnp.concatenate([lo, hi], axis=1).reshape(n_seq * seq, ATT_W)


def _q_two_maps(q_row):
    r = lax.broadcasted_iota(jnp.int32, (8, ATT_HW), 0)
    c = lax.broadcasted_iota(jnp.int32, (8, ATT_HW), 1)
    keep = ((r == 0) & (c < ATT_DH)) | ((r == 1) & (c >= ATT_DH))
    return jnp.where(keep, q_row * (ATT_DH ** -0.5), 0.0)


def _scores_kernel(pt_ref, q_ref, *refs, n_pages_step):
    k_refs = refs[:n_pages_step]
    s_ref = refs[n_pages_step]
    b = pl.program_id(0)
    for h in range(N_HEADS_ATT):
        q2 = _q_two_maps(q_ref[h, pl.ds(b, 1), :]).astype(BF16)
        for i in range(n_pages_step):
            s_ref[h, :, i * PAGE:(i + 1) * PAGE] = _dot_nt(q2, k_refs[i][h].astype(BF16))


def _page_specs(layer, n_pages_step):
    def pmap(i):
        return lambda b, c, pt: (layer, pt[b, c * n_pages_step + i], 0, 0, 0)
    return [pl.BlockSpec((None, None, N_HEADS_ATT, PAGE, ATT_HW), pmap(i)) for i in range(n_pages_step)]


def _attn_scores(q, cache_kt, page_table, layer, n_pages_step):
    bd, n_pages = page_table.shape
    steps = n_pages // n_pages_step
    grid_spec = pltpu.PrefetchScalarGridSpec(
        num_scalar_prefetch=1,
        grid=(bd, steps),
        in_specs=[pl.BlockSpec((N_HEADS_ATT, bd, ATT_HW), lambda b, c, pt: (0, 0, 0))]
        + _page_specs(layer, n_pages_step),
        out_specs=pl.BlockSpec((None, N_HEADS_ATT, 8, n_pages_step * PAGE),
                               lambda b, c, pt: (b, 0, 0, c)),
    )
    return pl.pallas_call(
        functools.partial(_scores_kernel, n_pages_step=n_pages_step),
        grid_spec=grid_spec,
        out_shape=jax.ShapeDtypeStruct((bd, N_HEADS_ATT, 8, n_pages * PAGE), F32),
        compiler_params=_cparams(("parallel", "arbitrary")),
        name="attn_scores",
    )(page_table, q, *([cache_kt] * n_pages_step))


def _attn_pv_kernel(pt_ref, s_ref, q_ref, kn_ref, vn_ref, al_ref, sub_ref, *refs,
                    n_pages_step, past, lam_init, slopes):
    v_refs = refs[:n_pages_step]
    o_ref = refs[n_pages_step]
    a_ref, anew_ref, acc_ref = refs[n_pages_step + 1:]
    b = pl.program_id(0)
    c = pl.program_id(1)

    @pl.when(c == 0)
    def _():
        lam = _lambda(al_ref, lam_init)
        dist = (past - lax.broadcasted_iota(jnp.int32, (1, past), 1)).astype(F32)
        for h in range(N_HEADS_ATT):
            s = s_ref[h] - slopes[h] * dist
            qk = _q_two_maps(q_ref[h, pl.ds(b, 1), :]) * kn_ref[h, pl.ds(b, 1), :]
            s_new = jnp.sum(qk, axis=-1, keepdims=True)
            m = jnp.maximum(jnp.max(s, axis=-1, keepdims=True), s_new)
            p = jnp.exp(s - m)
            p_new = jnp.exp(s_new - m)
            l = jnp.sum(p, axis=-1, keepdims=True) + p_new
            p = p / l
            p_new = p_new / l
            a_ref[h] = jnp.broadcast_to(p[0:1] - lam * p[1:2], (8, past))
            anew_ref[h] = jnp.broadcast_to(p_new[0:1] - lam * p_new[1:2], (8, ATT_HW))
        acc_ref[...] = jnp.zeros_like(acc_ref)

    for h in range(N_HEADS_ATT):
        acc = acc_ref[h]
        for i in range(n_pages_step):
            start = pl.multiple_of((c * n_pages_step + i) * PAGE, PAGE)
            a = a_ref[h, :, pl.ds(start, PAGE)].astype(BF16)
            acc = acc + _dot(a, v_refs[i][h].astype(BF16))
        acc_ref[h] = acc

    @pl.when(c == pl.num_programs(1) - 1)
    def _():
        for h in range(N_HEADS_ATT):
            o = acc_ref[h][0:1] + anew_ref[h][0:1] * vn_ref[h, pl.ds(b, 1), :]
            o_ref[:, h * ATT_HW:(h + 1) * ATT_HW] = _rms(o, sub_ref[...]) * (1.0 - lam_init)


def _attn_pv(scores, q, k_new, v_new, att_lambda, subln, cache_vt, page_table, layer,
             n_pages_step, lam_init):
    bd, n_pages = page_table.shape
    past = n_pages * PAGE
    steps = n_pages // n_pages_step
    const2 = lambda b, c, pt: (0, 0)
    const3 = lambda b, c, pt: (0, 0, 0)
    hm = pl.BlockSpec((N_HEADS_ATT, bd, ATT_HW), const3)
    grid_spec = pltpu.PrefetchScalarGridSpec(
        num_scalar_prefetch=1,
        grid=(bd, steps),
        in_specs=[
            pl.BlockSpec((None, N_HEADS_ATT, 8, past), lambda b, c, pt: (b, 0, 0, 0)),
            hm, hm, hm,
            pl.BlockSpec((4, ATT_DH), const2),
            pl.BlockSpec((1, ATT_HW), const2),
        ] + _page_specs(layer, n_pages_step),
        out_specs=pl.BlockSpec((None, 1, ATT_W), lambda b, c, pt: (b, 0, 0)),
        scratch_shapes=[pltpu.VMEM((N_HEADS_ATT, 8, past), F32),
                        pltpu.VMEM((N_HEADS_ATT, 8, ATT_HW), F32),
                        pltpu.VMEM((N_HEADS_ATT, 8, ATT_HW), F32)],
    )
    out = pl.pallas_call(
        functools.partial(_attn_pv_kernel, n_pages_step=n_pages_step, past=past,
                          lam_init=lam_init, slopes=[float(x) for x in _alibi_slopes(N_HEADS_ATT)]),
        grid_spec=grid_spec,
        out_shape=jax.ShapeDtypeStruct((bd, 1, ATT_W), F32),
        compiler_params=_cparams(("parallel", "arbitrary")),
        name="attn_pv",
    )(page_table, scores, q, k_new, v_new, att_lambda, subln, *([cache_vt] * n_pages_step))
    return out.reshape(bd, ATT_W)


def _mlstm_chunk_kernel(q_ref, k_ref, v_ref, o_ref, g_ref, bias_ref, norm_ref, c0_ref, n0_ref,
                        m0_ref, h_ref, c_ref, n_ref, m_ref, *, L):
    @pl.when(pl.program_id(1) == 0)
    def _():
        c_ref[...] = c0_ref[...]
        n_ref[...] = n0_ref[...]
        m_ref[...] = m0_ref[...]

    g = g_ref[...] + bias_ref[...]
    g_t = g.T
    r = lax.broadcasted_iota(jnp.int32, (L, L), 0)
    c = lax.broadcasted_iota(jnp.int32, (L, L), 1)
    causal = c <= r

    def lanes(x, n):
        reps = [x] * (n // 128) + ([x[:, :n % 128]] if n % 128 else [])
        return reps[0] if len(reps) == 1 else jnp.concatenate(reps, axis=1)

    for h in range(N_HEADS_ML):
        sl = slice(h * ML_DH, (h + 1) * ML_DH)
        ig_r = g_t[h:h + 1, :]
        lf_r = _log_sigmoid(g_t[N_HEADS_ML + h:N_HEADS_ML + h + 1, :])
        ig_c = jnp.broadcast_to(g[:, h:h + 1], (L, 128))
        lf_c = jnp.broadcast_to(_log_sigmoid(g[:, N_HEADS_ML + h:N_HEADS_ML + h + 1]), (L, 128))
        b_c = jnp.broadcast_to(jnp.sum(jnp.where(causal, lf_r, 0.0), axis=1, keepdims=True),
                               (L, 128))
        b_r = jnp.sum(jnp.where(r <= c, lanes(lf_c, L), 0.0), axis=0, keepdims=True)
        m_prev = m_ref[h:h + 1, :]
        d = jnp.where(causal, lanes(b_c, L) + (ig_r - b_r), -jnp.inf)
        inter = b_c + m_prev
        m_t = jnp.maximum(inter, jnp.max(d, axis=1, keepdims=True))
        w_intra = jnp.exp(d - lanes(m_t, L))
        w_state = jnp.exp(inter - m_t)

        q = q_ref[:, sl]
        k = k_ref[:, sl] * (ML_DH ** -0.5)
        v = v_ref[:, sl]
        qb, kb, vb = q.astype(BF16), k.astype(BF16), v.astype(BF16)
        cst = c_ref[h]
        nst = n_ref[h:h + 1, :]
        a = w_intra * _dot_nt(qb, kb)
        num = _dot(a.astype(BF16), vb) + lanes(w_state, ML_DH) * _dot_nt(qb, cst.astype(BF16))
        den = jnp.sum(a, axis=1, keepdims=True) + w_state * jnp.sum(q * nst, axis=1, keepdims=True)
        hh = num * lanes(1.0 / jnp.maximum(jnp.abs(den), jnp.exp(-m_t)), ML_DH)

        m_new = m_t[L - 1:L, :]
        b_last = b_c[L - 1:L, :]
        w_s = lanes(jnp.exp(b_last - b_c + ig_c - m_new), ML_DH)
        decay = lanes(jnp.exp(b_last + m_prev - m_new), ML_DH)
        c_ref[h] = decay * cst + _dot_tn((w_s * v).astype(BF16), kb)
        n_ref[h:h + 1, :] = decay * nst + jnp.sum(w_s * k, axis=0, keepdims=True)
        m_ref[h:h + 1, :] = m_new

        hn = _rms(hh, norm_ref[:, sl])
        h_ref[:, sl] = (_sigmoid(o_ref[:, sl]) * hn).astype(h_ref.dtype)


def _mlstm_prompt(proj, gates, b_if_pad, norm, c0, n0, m0, n_seq, seq, L):
    nc = seq // L
    blk = lambda off: pl.BlockSpec((L, ML_W), lambda b, c, off=off: (b * nc + c, off // ML_W))
    const2 = lambda b, c: (0, 0)
    return pl.pallas_call(
        functools.partial(_mlstm_chunk_kernel, L=L),
        grid=(n_seq, nc),
        in_specs=[
            blk(R_MQ), blk(R_MK), blk(R_MV), blk(R_MO),
            pl.BlockSpec((L, GATE_W), lambda b, c: (b * nc + c, 0)),
            pl.BlockSpec((1, GATE_W), const2),
            pl.BlockSpec((1, ML_W), const2),
            pl.BlockSpec((None, N_HEADS_ML, ML_DH, ML_DH), lambda b, c: (b, 0, 0, 0)),
            pl.BlockSpec((None, N_HEADS_ML, ML_DH), lambda b, c: (b, 0, 0)),
            pl.BlockSpec((None, N_HEADS_ML, 128), lambda b, c: (b, 0, 0)),
        ],
        out_specs=[
            pl.BlockSpec((L, ML_W), lambda b, c: (b * nc + c, 0)),
            pl.BlockSpec((None, N_HEADS_ML, ML_DH, ML_DH), lambda b, c: (b, 0, 0, 0)),
            pl.BlockSpec((None, N_HEADS_ML, ML_DH), lambda b, c: (b, 0, 0)),
            pl.BlockSpec((None, N_HEADS_ML, 128), lambda b, c: (b, 0, 0)),
        ],
        out_shape=[
            jax.ShapeDtypeStruct((n_seq * seq, ML_W), BF16),
            jax.ShapeDtypeStruct((n_seq, N_HEADS_ML, ML_DH, ML_DH), F32),
            jax.ShapeDtypeStruct((n_seq, N_HEADS_ML, ML_DH), F32),
            jax.ShapeDtypeStruct((n_seq, N_HEADS_ML, 128), F32),
        ],
        compiler_params=_cparams(("parallel", "arbitrary")),
        name="mlstm_prompt",
    )(proj, proj, proj, proj, gates, b_if_pad, norm, c0, n0, m0)


def _mlstm_step_kernel(q_ref, k_ref, v_ref, o_ref, g_ref, bias_ref, norm_ref, c0_ref, n0_ref,
                       m0_ref, h_ref, c_ref, n_ref, m_ref):
    b = pl.program_id(0)
    g = g_ref[pl.ds(b, 1), :] + bias_ref[...]
    r = lax.broadcasted_iota(jnp.int32, (ML_DH, ML_DH), 0)
    c = lax.broadcasted_iota(jnp.int32, (ML_DH, ML_DH), 1)
    eye = r == c
    for h in range(N_HEADS_ML):
        sl = slice(h * ML_DH, (h + 1) * ML_DH)
        ig = g[:, h:h + 1]
        lf = _log_sigmoid(g[:, N_HEADS_ML + h:N_HEADS_ML + h + 1])
        m_prev = m0_ref[h:h + 1, 0:1]
        inter = lf + m_prev
        m_t = jnp.maximum(inter, ig)
        w_i = jnp.exp(ig - m_t)
        w_state = jnp.exp(inter - m_t)

        q = q_ref[pl.ds(b, 1), sl]
        k = k_ref[pl.ds(b, 1), sl] * (ML_DH ** -0.5)
        v = v_ref[pl.ds(b, 1), sl]
        cst = c0_ref[h]
        nst = n0_ref[h:h + 1, :]
        a = w_i * jnp.sum(q * k, axis=1, keepdims=True)
        cq = _dot_nt(jnp.broadcast_to(q, (8, ML_DH)).astype(BF16), cst.astype(BF16))[0:1]
        num = a * v + w_state * cq
        den = a + w_state * jnp.sum(q * nst, axis=1, keepdims=True)
        hh = num / jnp.maximum(jnp.abs(den), jnp.exp(-m_t))

        v_col = jnp.sum(jnp.where(eye, w_i * v, 0.0), axis=1, keepdims=True)
        c_ref[h] = w_state * cst + v_col * k
        n_ref[h:h + 1, :] = w_state * nst + w_i * k
        m_ref[h:h + 1, :] = jnp.broadcast_to(m_t, (1, 128))

        hn = _rms(hh, norm_ref[:, sl])
        h_ref[:, sl] = _sigmoid(o_ref[pl.ds(b, 1), sl]) * hn


def _mlstm_step(proj, gates, b_if_pad, norm, c0, n0, m0):
    bd = proj.shape[0]
    blk = lambda off: pl.BlockSpec((bd, ML_W), lambda b, off=off: (0, off // ML_W))
    const2 = lambda b: (0, 0)
    st4 = pl.BlockSpec((None, N_HEADS_ML, ML_DH, ML_DH), lambda b: (b, 0, 0, 0))
    st3 = pl.BlockSpec((None, N_HEADS_ML, ML_DH), lambda b: (b, 0, 0))
    stm = pl.BlockSpec((None, N_HEADS_ML, 128), lambda b: (b, 0, 0))
    out = pl.pallas_call(
        _mlstm_step_kernel,
        grid=(bd,),
        in_specs=[blk(R_MQ), blk(R_MK), blk(R_MV), blk(R_MO),
                  pl.BlockSpec((bd, GATE_W), const2),
                  pl.BlockSpec((1, GATE_W), const2),
                  pl.BlockSpec((1, ML_W), const2),
                  st4, st3, stm],
        out_specs=[pl.BlockSpec((None, 1, ML_W), lambda b: (b, 0, 0)), st4, st3, stm],
        out_shape=[
            jax.ShapeDtypeStruct((bd, 1, ML_W), F32),
            jax.ShapeDtypeStruct((bd, N_HEADS_ML, ML_DH, ML_DH), F32),
            jax.ShapeDtypeStruct((bd, N_HEADS_ML, ML_DH), F32),
            jax.ShapeDtypeStruct((bd, N_HEADS_ML, 128), F32),
        ],
        compiler_params=_cparams(("parallel",)),
        name="mlstm_step",
    )(proj, proj, proj, proj, gates, b_if_pad, norm, c0, n0, m0)
    return (out[0].reshape(bd, ML_W),) + tuple(out[1:])


def _conv3_rows(u, w_ref, prev8):
    w0, w1, w2 = w_ref[0:1, :], w_ref[1:2, :], w_ref[2:3, :]
    y = w0 * pltpu.roll(u, 2, 0) + w1 * pltpu.roll(u, 1, 0) + w2 * u
    head = u[0:8, :]
    row = lax.broadcasted_iota(jnp.int32, head.shape, 0)
    p1 = prev8[7:8, :]
    p2 = prev8[6:7, :]
    h1 = jnp.where(row == 0, p1, pltpu.roll(head, 1, 0))
    h2 = jnp.where(row == 0, p2, jnp.where(row == 1, p1, pltpu.roll(head, 2, 0)))
    return jnp.concatenate([w0 * h2 + w1 * h1 + w2 * head, y[8:, :]], axis=0)


def _conv3_state(u, w_ref, s0, s1):
    return w_ref[0:1, :] * s0 + w_ref[1:2, :] * s1 + w_ref[2:3, :] * u


def _outproj_prompt_kernel(x_ref, a_ref, hm_ref, cb_ref, cc_ref, cx_ref, cw_ref, w_ref,
                           o_ref, st_ref, carry_ref, *, tiles_per_seq):
    i = pl.program_id(0)
    u = cc_ref[...] * cx_ref[...]
    prev8 = jnp.where(i % tiles_per_seq == 0, 0.0, carry_ref[...])
    yc = cb_ref[...] * _conv3_rows(u, cw_ref, prev8)
    tail = u[u.shape[0] - 8:, :]
    carry_ref[...] = tail
    st_ref[...] = tail
    acc = _dot(a_ref[...], w_ref[0:ATT_W, :])
    acc = acc + _dot(hm_ref[...], w_ref[ATT_W:ATT_W + ML_W, :])
    acc = acc + _dot(yc.astype(BF16), w_ref[ATT_W + ML_W:, :])
    o_ref[...] = x_ref[...] + acc


def _outproj_prompt(x, a, hm, proj, conv_w, w_out, wl, n_seq, seq, tm):
    m = x.shape[0]
    tps = seq // tm
    cblk = lambda off: pl.BlockSpec((tm, CONV_C), lambda i, off=off: (i, off // CONV_C))
    return pl.pallas_call(
        functools.partial(_outproj_prompt_kernel, tiles_per_seq=tps),
        grid=(m // tm,),
        in_specs=[
            pl.BlockSpec((tm, D_MODEL), lambda i: (i, 0)),
            pl.BlockSpec((tm, ATT_W), lambda i: (i, 0)),
            pl.BlockSpec((tm, ML_W), lambda i: (i, 0)),
            cblk(R_CB), cblk(R_CC), cblk(R_CX),
            pl.BlockSpec((3, CONV_C), lambda i: (0, 0)),
            pl.BlockSpec((None, D_MODEL, D_MODEL), lambda i: (wl, 0, 0)),
        ],
        out_specs=[
            pl.BlockSpec((tm, D_MODEL), lambda i: (i, 0)),
            pl.BlockSpec((None, 8, CONV_C), lambda i: (i, 0, 0)),
        ],
        out_shape=[
            jax.ShapeDtypeStruct((m, D_MODEL), F32),
            jax.ShapeDtypeStruct((m // tm, 8, CONV_C), F32),
        ],
        scratch_shapes=[pltpu.VMEM((8, CONV_C), F32)],
        compiler_params=_cparams(("arbitrary",)),
        name="outproj_prompt",
    )(x, a, hm, proj, proj, proj, conv_w, w_out)


def _outproj_step_kernel(x_ref, a_ref, hm_ref, cb_ref, cc_ref, cx_ref, cw_ref, s0_ref, s1_ref,
                         w_ref, o_ref, u_ref):
    u = cc_ref[...] * cx_ref[...]
    u_ref[...] = u
    yc = cb_ref[...] * _conv3_state(u, cw_ref, s0_ref[...], s1_ref[...])
    acc = _dot(a_ref[...].astype(BF16), w_ref[0:ATT_W, :])
    acc = acc + _dot(hm_ref[...].astype(BF16), w_ref[ATT_W:ATT_W + ML_W, :])
    acc = acc + _dot(yc.astype(BF16), w_ref[ATT_W + ML_W:, :])
    o_ref[...] = x_ref[...] + acc


def _outproj_step(x, a, hm, proj, conv_w, s0, s1, w_out, wl):
    bd = x.shape[0]
    full = lambda shp: pl.BlockSpec(shp, lambda i: (0, 0))
    cblk = lambda off: pl.BlockSpec((bd, CONV_C), lambda i, off=off: (0, off // CONV_C))
    return pl.pallas_call(
        _outproj_step_kernel,
        grid=(1,),
        in_specs=[full((bd, D_MODEL)), full((bd, ATT_W)), full((bd, ML_W)),
                  cblk(R_CB), cblk(R_CC), cblk(R_CX),
                  full((3, CONV_C)), full((bd, CONV_C)), full((bd, CONV_C)),
                  pl.BlockSpec((None, D_MODEL, D_MODEL), lambda i: (wl, 0, 0))],
        out_specs=[full((bd, D_MODEL)), full((bd, CONV_C))],
        out_shape=[jax.ShapeDtypeStruct((bd, D_MODEL), F32),
                   jax.ShapeDtypeStruct((bd, CONV_C), F32)],
        compiler_params=_cparams(("arbitrary",)),
        name="outproj_step",
    )(x, a, hm, proj, proj, proj, conv_w, s0, s1, w_out)


def _ffn_prompt_kernel(x_ref, g_ref, wg_ref, wv_ref, cwg_ref, cwv_ref, wd_ref, gfin_ref,
                       o_ref, stg_ref, stv_ref, h_ref, carry_ref, *, tiles_per_seq, final_norm):
    i = pl.program_id(0)
    j = pl.program_id(1)

    @pl.when(j == 0)
    def _():
        x = x_ref[...]
        h_ref[...] = _rms(x, g_ref[...]).astype(BF16)
        o_ref[...] = x

    h = h_ref[...]
    ug = _dot(h, wg_ref[...])
    uv = _dot(h, wv_ref[...])
    start = i % tiles_per_seq == 0
    pg = jnp.where(start, 0.0, carry_ref[j, 0])
    pv = jnp.where(start, 0.0, carry_ref[j, 1])
    cg = _conv3_rows(ug, cwg_ref, pg)
    cv = _conv3_rows(uv, cwv_ref, pv)
    tm = ug.shape[0]
    tg = ug[tm - 8:, :]
    tv = uv[tm - 8:, :]
    carry_ref[j, 0] = tg
    carry_ref[j, 1] = tv
    stg_ref[...] = tg
    stv_ref[...] = tv
    act = (cg * _sigmoid(cg) * cv).astype(BF16)
    o_ref[...] += _dot(act, wd_ref[...])
    if final_norm:
        @pl.when(j == pl.num_programs(1) - 1)
        def _():
            o_ref[...] = _rms(o_ref[...], gfin_ref[...])


def _ffn_prompt(x, g, w_up, ffn_conv_w, w_down, wl, g_final, final_norm, n_seq, seq, tm, tf):
    m = x.shape[0]
    tps = seq // tm
    nj = D_FF // tf
    return pl.pallas_call(
        functools.partial(_ffn_prompt_kernel, tiles_per_seq=tps, final_norm=final_norm),
        grid=(m // tm, nj),
        in_specs=[
            pl.BlockSpec((tm, D_MODEL), lambda i, j: (i, 0)),
            pl.BlockSpec((1, D_MODEL), lambda i, j: (0, 0)),
            pl.BlockSpec((None, D_MODEL, tf), lambda i, j: (wl, 0, j)),
            pl.BlockSpec((None, D_MODEL, tf), lambda i, j: (wl, 0, nj + j)),
            pl.BlockSpec((3, tf), lambda i, j: (0, j)),
            pl.BlockSpec((3, tf), lambda i, j: (0, nj + j)),
            pl.BlockSpec((None, tf, D_MODEL), lambda i, j: (wl, j, 0)),
            pl.BlockSpec((1, D_MODEL), lambda i, j: (0, 0)),
        ],
        out_specs=[
            pl.BlockSpec((tm, D_MODEL), lambda i, j: (i, 0)),
            pl.BlockSpec((None, 8, tf), lambda i, j: (i, 0, j)),
            pl.BlockSpec((None, 8, tf), lambda i, j: (i, 0, j)),
        ],
        out_shape=[
            jax.ShapeDtypeStruct((m, D_MODEL), F32),
            jax.ShapeDtypeStruct((m // tm, 8, D_FF), F32),
            jax.ShapeDtypeStruct((m // tm, 8, D_FF), F32),
        ],
        scratch_shapes=[pltpu.VMEM((tm, D_MODEL), BF16), pltpu.VMEM((nj, 2, 8, tf), F32)],
        compiler_params=_cparams(("arbitrary", "arbitrary")),
        name="ffn_prompt",
    )(x, g, w_up, w_up, ffn_conv_w, ffn_conv_w, w_down, g_final)


def _ffn_step_kernel(x_ref, g_ref, wg_ref, wv_ref, cwg_ref, cwv_ref, s0g_ref, s1g_ref, s0v_ref,
                     s1v_ref, wd_ref, gfin_ref, o_ref, ug_ref, uv_ref, h_ref, *, final_norm):
    @pl.when(pl.program_id(0) == 0)
    def _():
        x = x_ref[...]
        h_ref[...] = _rms(x, g_ref[...]).astype(BF16)
        o_ref[...] = x

    h = h_ref[...]
    ug = _dot(h, wg_ref[...])
    uv = _dot(h, wv_ref[...])
    ug_ref[...] = ug
    uv_ref[...] = uv
    cg = _conv3_state(ug, cwg_ref, s0g_ref[...], s1g_ref[...])
    cv = _conv3_state(uv, cwv_ref, s0v_ref[...], s1v_ref[...])
    act = (cg * _sigmoid(cg) * cv).astype(BF16)
    o_ref[...] += _dot(act, wd_ref[...])
    if final_norm:
        @pl.when(pl.program_id(0) == pl.num_programs(0) - 1)
        def _():
            o_ref[...] = _rms(o_ref[...], gfin_ref[...])


def _ffn_step(x, g, w_up, ffn_conv_w, s0, s1, w_down, wl, g_final, final_norm, tf):
    bd = x.shape[0]
    nj = D_FF // tf
    gcol = lambda rows: pl.BlockSpec((rows, tf), lambda j: (0, j))
    vcol = lambda rows: pl.BlockSpec((rows, tf), lambda j: (0, nj + j))
    return pl.pallas_call(
        functools.partial(_ffn_step_kernel, final_norm=final_norm),
        grid=(nj,),
        in_specs=[
            pl.BlockSpec((bd, D_MODEL), lambda j: (0, 0)),
            pl.BlockSpec((1, D_MODEL), lambda j: (0, 0)),
            pl.BlockSpec((None, D_MODEL, tf), lambda j: (wl, 0, j)),
            pl.BlockSpec((None, D_MODEL, tf), lambda j: (wl, 0, nj + j)),
            gcol(3), vcol(3),
            gcol(bd), gcol(bd), vcol(bd), vcol(bd),
            pl.BlockSpec((None, tf, D_MODEL), lambda j: (wl, j, 0)),
            pl.BlockSpec((1, D_MODEL), lambda j: (0, 0)),
        ],
        out_specs=[pl.BlockSpec((bd, D_MODEL), lambda j: (0, 0)), gcol(bd), gcol(bd)],
        out_shape=[jax.ShapeDtypeStruct((bd, D_MODEL), F32),
                   jax.ShapeDtypeStruct((bd, D_FF), F32),
                   jax.ShapeDtypeStruct((bd, D_FF), F32)],
        scratch_shapes=[pltpu.VMEM((bd, D_MODEL), BF16)],
        compiler_params=_cparams(("arbitrary",)),
        name="ffn_step",
    )(x, g, w_up, w_up, ffn_conv_w, ffn_conv_w, s0, s1, s0, s1, w_down, g_final)


def _tiles(seq):
    tm = min(512, seq)
    tmi = min(1024, seq)
    tf = 512
    tq = min(256, seq)
    rs = tq
    lc = min(512, seq)
    return tm, tmi, tf, tq, rs, lc


def kernel(x_prompt, x_sample, cache_k, cache_v, state_mlstm_c, state_mlstm_n, state_mlstm_m,
           state_conv, state_ffn_conv, page_table,
           g_mix, w_in, b_if, att_lambda, att_subln, mlstm_norm, conv_w, w_out,
           g_ffn, w_up, ffn_conv_w, w_down, g_final):
    depth = w_in.shape[0]
    n_seq, seq, _ = x_prompt.shape
    bd = x_sample.shape[0]
    n_pool = cache_k.shape[1]
    tm, tmi, tf, tq, rs, lc = _tiles(seq)
    tps = seq // tm
    n_pages = page_table.shape[1]
    pages_step = math.gcd(32, n_pages)

    n_gate = 2 * N_HEADS_ML
    w_bf = w_in.transpose(0, 2, 1).astype(BF16)
    w_conv = w_bf[:, N_QKV_MLSTM + n_gate:, :]
    w_out_b = w_out.astype(BF16)
    w_up_b = w_up.astype(BF16)
    w_down_b = w_down.astype(BF16)
    b_if_pad = jnp.pad(b_if, ((0, 0), (0, GATE_W - n_gate))).reshape(depth, 1, GATE_W)

    slopes = jnp.asarray(_alibi_slopes(N_HEADS_ATT))
    cache_kt = cache_k.transpose(0, 1, 3, 2, 4)
    cache_vt = cache_v.transpose(0, 1, 3, 2, 4)

    c0p = jnp.zeros((n_seq, N_HEADS_ML, ML_DH, ML_DH), F32)
    n0p = jnp.zeros((n_seq, N_HEADS_ML, ML_DH), F32)
    m0p = jnp.full((n_seq, N_HEADS_ML, 128), -jnp.inf, F32)

    xp = x_prompt.reshape(n_seq * seq, D_MODEL)
    xs = x_sample.reshape(bd, D_MODEL)
    outs_p = [[] for _ in range(7)]
    outs_s = [[] for _ in range(7)]
    k_all = v_all = None
    gfin = g_final.reshape(1, D_MODEL)
    for l in range(depth):
        last = l == depth - 1
        lam_init = 0.8 - 0.6 * math.exp(-0.3 * l)
        gm = g_mix[l].reshape(1, D_MODEL)
        gf = g_ffn[l].reshape(1, D_MODEL)
        sub = att_subln[l].reshape(1, ATT_HW)
        mnorm = mlstm_norm[l].reshape(1, ML_W)

        proj, gates, q, k_all, v_all = _inproj(xp, gm, w_bf, w_conv, l, n_seq, seq, tmi, l, depth,
                                               k_all, v_all)
        a = _attn_prompt(q, k_all, v_all, l, slopes, att_lambda[l], sub, n_seq, seq, tq, rs, lam_init)
        hm, c1, n1, m1 = _mlstm_prompt(proj, gates, b_if_pad[l], mnorm, c0p, n0p, m0p, n_seq, seq, lc)
        xp, cst = _outproj_prompt(xp, a, hm, proj, conv_w[l], w_out_b, l, n_seq, seq, tm)
        xp, stg, stv = _ffn_prompt(xp, gf, w_up_b, ffn_conv_w[l], w_down_b, l, gfin, last,
                                   n_seq, seq, tm, tf)
        outs_p[2].append(c1)
        outs_p[3].append(n1)
        outs_p[4].append(m1[:, :, 0])
        outs_p[5].append(cst[tps - 1::tps, 6:8, :])
        outs_p[6].append(jnp.concatenate([stg[tps - 1::tps, 6:8, :], stv[tps - 1::tps, 6:8, :]], axis=-1))

        proj_s, gates_s, q_s, k_s, v_s = _inproj(xs, gm, w_bf, w_conv, l, 1, bd, bd, 0, 1)
        q_s, k_s, v_s = q_s[0], k_s[0, 0], v_s[0, 0]
        scores = _attn_scores(q_s, cache_kt, page_table, l, pages_step)
        a_s = _attn_pv(scores, q_s, k_s, v_s, att_lambda[l], sub, cache_vt, page_table, l,
                       pages_step, lam_init)
        m0s = jnp.broadcast_to(state_mlstm_m[l][:, :, None], (bd, N_HEADS_ML, 128))
        hm_s, c1s, n1s, m1s = _mlstm_step(proj_s, gates_s, b_if_pad[l], mnorm, state_mlstm_c[l],
                                          state_mlstm_n[l], m0s)
        sc = state_conv[l]
        xs, u_s = _outproj_step(xs, a_s, hm_s, proj_s, conv_w[l], sc[:, 0], sc[:, 1], w_out_b, l)
        sf = state_ffn_conv[l]
        xs, ug_s, uv_s = _ffn_step(xs, gf, w_up_b, ffn_conv_w[l], sf[:, 0], sf[:, 1], w_down_b, l,
                                   gfin, last, tf)
        outs_s[0].append(k_s.transpose(1, 0, 2).reshape(bd, 1, N_HEADS_ATT, ATT_HW))
        outs_s[1].append(v_s.transpose(1, 0, 2).reshape(bd, 1, N_HEADS_ATT, ATT_HW))
        outs_s[2].append(c1s)
        outs_s[3].append(n1s)
        outs_s[4].append(m1s[:, :, 0])
        outs_s[5].append(jnp.stack([sc[:, 1], u_s], axis=1))
        outs_s[6].append(jnp.stack([sf[:, 1], jnp.concatenate([ug_s, uv_s], axis=-1)], axis=1))

    y_prompt = xp.reshape(n_seq, seq, D_MODEL)
    y_sample = xs.reshape(bd, 1, D_MODEL)
    c_p, n_p, m_p, conv_p, ffn_p = [jnp.stack(o, 0) for o in outs_p[2:]]
    k_p = k_all.transpose(0, 1, 3, 2, 4)
    v_p = v_all.transpose(0, 1, 3, 2, 4)
    k_s_, v_s_, c_s, n_s, m_s, conv_s, ffn_s = [jnp.stack(o, 0) for o in outs_s]
    return (y_prompt, y_sample, k_p, v_p, k_s_, v_s_, c_p, n_p, m_p, c_s, n_s, m_s,
            conv_p, conv_s, ffn_p, ffn_s)
```

```python
import functools
import math

import numpy as np
import jax
import jax.numpy as jnp
from jax import lax
from jax.experimental import pallas as pl
from jax.experimental.pallas import tpu as pltpu

F32 = jnp.float32
BF16 = jnp.bfloat16

D_MODEL = 2048
N_HEADS_ATT = 6
ATT_DH = 64
ATT_HW = 2 * ATT_DH
ATT_W = N_HEADS_ATT * ATT_HW
N_HEADS_ML = 4
ML_DH = 192
ML_W = N_HEADS_ML * ML_DH
CONV_C = D_MODEL - ATT_W - ML_W
D_FF = 5632
PAGE = 128
EPS = 1e-6
LOG2E = 1.4426950408889634

GATE_W = 128

VMEM_LIMIT = 56 * 1024 * 1024


def _cparams(sem):
    return pltpu.CompilerParams(dimension_semantics=sem, vmem_limit_bytes=VMEM_LIMIT)


def _alibi_slopes(n_heads):
    def geometric(n):
        start = 2.0 ** (-8.0 / n)
        return [start ** (i + 1) for i in range(n)]
    closest = 2 ** int(math.floor(math.log2(n_heads)))
    slopes = geometric(closest)
    if closest < n_heads:
        slopes = slopes + geometric(2 * closest)[0::2][: n_heads - closest]
    return np.asarray(slopes, dtype=np.float32)


def _rms(x, g):
    return x * lax.rsqrt(jnp.mean(x * x, axis=-1, keepdims=True) + EPS) * g


def _log_sigmoid(x):
    return jnp.minimum(x, 0.0) - jnp.log1p(jnp.exp(-jnp.abs(x)))


def _sigmoid(x):
    return 0.5 * (jnp.tanh(0.5 * x) + 1.0)


def _lambda(al_ref, lam_init):
    al = al_ref[...]
    s01 = jnp.sum(al[0:1] * al[1:2], axis=-1, keepdims=True)
    s23 = jnp.sum(al[2:3] * al[3:4], axis=-1, keepdims=True)
    return jnp.exp(s01) - jnp.exp(s23) + lam_init


def _dot_nt(a, b):
    return lax.dot_general(a, b, (((1,), (1,)), ((), ())), preferred_element_type=F32)


def _dot_tn(a, b):
    return lax.dot_general(a, b, (((0,), (0,)), ((), ())), preferred_element_type=F32)


def _dot(a, b):
    return jnp.dot(a, b, preferred_element_type=F32)


TN = ATT_W
N_QKV_MLSTM = 3 * ATT_W + 4 * ML_W
J_ML = 3
J_CONV = N_QKV_MLSTM // TN
N_REST = 4 * ML_W + 3 * CONV_C
R_MQ, R_MK, R_MV, R_MO = 0, ML_W, 2 * ML_W, 3 * ML_W
R_CB = 4 * ML_W
R_CC, R_CX = R_CB + CONV_C, R_CB + 2 * CONV_C


def _inproj_kernel(*refs, layer_slot):
    if layer_slot:
        refs = refs[2:]
    x_ref, g_ref, w_ref, wc_ref, wg_ref, rest_ref, og_ref, q_ref, k_ref, v_ref, h_ref = refs
    j = pl.program_id(2)

    @pl.when(j == 0)
    def _():
        h = _rms(x_ref[...], g_ref[...]).astype(BF16)
        h_ref[...] = h
        og_ref[...] = _dot_nt(h, wg_ref[...])

    for t, ref in enumerate((q_ref, k_ref, v_ref)):
        @pl.when(j == t)
        def _(ref=ref):
            res = _dot_nt(h_ref[...], w_ref[...])
            for h in range(N_HEADS_ATT):
                ref[h] = res[:, h * ATT_HW:(h + 1) * ATT_HW]

    @pl.when((j >= J_ML) & (j < J_CONV))
    def _():
        rest_ref[...] = _dot_nt(h_ref[...], w_ref[...])

    @pl.when(j >= J_CONV)
    def _():
        rest_ref[...] = _dot_nt(h_ref[...], wc_ref[...])


def _inproj(x, g, w_bf, w_conv, wl, n_seq, seq, tm, layer, depth, k_all=None, v_all=None):
    m = x.shape[0]
    tps = seq // tm
    nj = J_CONV + 3 * CONV_C // TN
    alias = k_all is not None
    rest_map = lambda b, i, j: (b * tps + i, jnp.maximum(j - J_ML, 0))
    row = lambda b, i, j: (b * tps + i, 0)
    const = lambda b, i, j: (0, 0)
    kv_spec = pl.BlockSpec((None, None, N_HEADS_ATT, tm, ATT_HW), lambda b, i, j: (layer, b, 0, i, 0))
    in_specs = [
        pl.BlockSpec((tm, D_MODEL), row, pipeline_mode=pl.Buffered(1)),
        pl.BlockSpec((1, D_MODEL), const),
        pl.BlockSpec((None, TN, D_MODEL), lambda b, i, j: (wl, jnp.minimum(j, J_CONV - 1), 0)),
        pl.BlockSpec((None, TN, D_MODEL), lambda b, i, j: (wl, jnp.maximum(j - J_CONV, 0), 0)),
        pl.BlockSpec((None, GATE_W, D_MODEL), lambda b, i, j: (wl, N_QKV_MLSTM // GATE_W, 0)),
    ]
    args = [x, g, w_bf, w_conv, w_bf]
    if alias:
        in_specs = [pl.BlockSpec(memory_space=pl.ANY), pl.BlockSpec(memory_space=pl.ANY)] + in_specs
        args = [k_all, v_all] + args
    kv_shape = jax.ShapeDtypeStruct((depth, n_seq, N_HEADS_ATT, seq, ATT_HW), F32)
    return pl.pallas_call(
        functools.partial(_inproj_kernel, layer_slot=alias),
        grid=(n_seq, tps, nj),
        in_specs=in_specs,
        out_specs=[
            pl.BlockSpec((tm, TN), rest_map),
            pl.BlockSpec((tm, GATE_W), row),
            pl.BlockSpec((None, N_HEADS_ATT, tm, ATT_HW), lambda b, i, j: (b, 0, i, 0)),
            kv_spec, kv_spec,
        ],
        out_shape=[
            jax.ShapeDtypeStruct((m, N_REST), F32),
            jax.ShapeDtypeStruct((m, GATE_W), F32),
            jax.ShapeDtypeStruct((n_seq, N_HEADS_ATT, seq, ATT_HW), F32),
            kv_shape, kv_shape,
        ],
        scratch_shapes=[pltpu.VMEM((tm, D_MODEL), BF16)],
        input_output_aliases={0: 3, 1: 4} if alias else {},
        compiler_params=_cparams(("parallel", "parallel", "arbitrary")),
        name="inproj",
    )(*args)


def _attn_prompt_kernel(slopes_ref, qa_ref, qb_ref, k_ref, v_ref, al_ref, sub_ref, oa_ref, ob_ref,
                        q2_ref, kb_ref, vb_ref, m_ref, l_ref, acc_ref, *, tq, rs, nq, lam_init):
    h = pl.program_id(1)
    i = pl.program_id(2)
    slope = slopes_ref[h]
    tile_a, tile_b = i, nq - 1 - i

    @pl.when(i == 0)
    def _():
        kb_ref[...] = k_ref[...].astype(BF16)
        vb_ref[...] = v_ref[...].astype(BF16)

    lane = lax.broadcasted_iota(jnp.int32, (tq, ATT_HW), 1)
    for n, ref in enumerate((qa_ref, qb_ref)):
        q = ref[...] * (ATT_DH ** -0.5 * LOG2E)
        q2_ref[2 * n * tq:(2 * n + 1) * tq, :] = jnp.where(lane < ATT_DH, q, 0.0).astype(BF16)
        q2_ref[(2 * n + 1) * tq:(2 * n + 2) * tq, :] = jnp.where(lane >= ATT_DH, q, 0.0).astype(BF16)
    m_ref[...] = jnp.full_like(m_ref, -jnp.inf)
    l_ref[...] = jnp.zeros_like(l_ref)
    acc_ref[...] = jnp.zeros_like(acc_ref)
    col = lax.broadcasted_iota(jnp.int32, (1, tq), 1)

    def block(row_base, tile, kblk, diag):
        start = pl.multiple_of(kblk * tq, tq)
        bias = (slope * LOG2E) * (col + (kblk - tile) * tq).astype(F32)
        for t in range(2 * tq // rs):
            r0 = row_base + t * rs
            rows = pl.ds(r0 if isinstance(r0, int) else pl.multiple_of(r0, rs), rs)
            q0 = (t * rs) % tq
            nk = min(tq, -(-(q0 + rs) // ATT_HW) * ATT_HW) if diag else tq
            k = kb_ref[pl.ds(start, nk), :]
            v = vb_ref[pl.ds(start, nk), :]
            s = _dot_nt(q2_ref[rows, :], k) + bias[:, :nk]
            if diag:
                r = lax.broadcasted_iota(jnp.int32, (rs, nk), 0) + q0
                c = lax.broadcasted_iota(jnp.int32, (rs, nk), 1)
                s = jnp.where(c <= r, s, -jnp.inf)
            m_old = m_ref[rows, :]
            m_new = jnp.maximum(m_old, jnp.max(s, axis=-1, keepdims=True))
            alpha = jnp.exp2(m_old - m_new)
            p = jnp.exp2(s - jnp.tile(m_new, (1, nk // ATT_HW)))
            l_ref[rows, :] = alpha * l_ref[rows, :] + jnp.sum(p, axis=-1, keepdims=True)
            acc_ref[rows, :] = alpha * acc_ref[rows, :] + _dot(p.astype(BF16), v)
            m_ref[rows, :] = m_new

    for u in range(nq - 1):
        to_b = (u >= i).astype(jnp.int32)
        block(to_b * (2 * tq), jnp.where(u >= i, tile_b, tile_a), u - to_b * i, False)
    block(0, tile_a, tile_a, True)
    block(2 * tq, tile_b, tile_b, True)

    lam = _lambda(al_ref, lam_init)
    o = acc_ref[...] / l_ref[...]
    for n, ref in enumerate((oa_ref, ob_ref)):
        on = o[2 * n * tq:(2 * n + 1) * tq] - lam * o[(2 * n + 1) * tq:(2 * n + 2) * tq]
        ref[...] = (_rms(on, sub_ref[...]) * (1.0 - lam_init)).astype(ref.dtype)


def _attn_prompt(q, k_all, v_all, layer, slopes, att_lambda, subln, n_seq, seq, tq, rs, lam_init):
    nq = seq // tq
    half = nq // 2
    w = ATT_HW
    kv_spec = pl.BlockSpec((None, None, None, seq, w), lambda b, h, i: (layer, b, h, 0, 0))
    o_shape = jax.ShapeDtypeStruct((n_seq, half, tq, ATT_W), BF16)
    lo, hi = pl.pallas_call(
        functools.partial(_attn_prompt_kernel, tq=tq, rs=rs, nq=nq, lam_init=lam_init),
        grid=(n_seq, N_HEADS_ATT, half),
        in_specs=[
            pl.BlockSpec(memory_space=pltpu.SMEM),
            pl.BlockSpec((None, None, tq, w), lambda b, h, i: (b, h, i, 0)),
            pl.BlockSpec((None, None, tq, w), lambda b, h, i: (b, h, nq - 1 - i, 0)),
            kv_spec, kv_spec,
            pl.BlockSpec((4, ATT_DH), lambda b, h, i: (0, 0)),
            pl.BlockSpec((1, w), lambda b, h, i: (0, 0)),
        ],
        out_specs=[pl.BlockSpec((None, None, tq, w), lambda b, h, i: (b, i, 0, h)),
                   pl.BlockSpec((None, None, tq, w), lambda b, h, i: (b, half - 1 - i, 0, h))],
        out_shape=[o_shape, o_shape],
        scratch_shapes=[pltpu.VMEM((4 * tq, w), BF16), pltpu.VMEM((seq, w), BF16),
                        pltpu.VMEM((seq, w), BF16), pltpu.VMEM((4 * tq, w), F32),
                        pltpu.VMEM((4 * tq, w), F32), pltpu.VMEM((4 * tq, w), F32)],
        compiler_params=_cparams(("parallel", "parallel", "arbitrary")),
        name="attn_prompt",
    )(slopes, q, q, k_all, v_all, att_lambda, subln)
    return jnp.concatenate([lo, hi], axis=1).reshape(n_seq * seq, ATT_W)


def _q_two_maps(q_row):
    r = lax.broadcasted_iota(jnp.int32, (8, ATT_HW), 0)
    c = lax.broadcasted_iota(jnp.int32, (8, ATT_HW), 1)
    keep = ((r == 0) & (c < ATT_DH)) | ((r == 1) & (c >= ATT_DH))
    return jnp.where(keep, q_row * (ATT_DH ** -0.5), 0.0)


def _q_rows(q_ref, b):
    return jnp.concatenate([_q_two_maps(q_ref[h, pl.ds(b, 1), :]) for h in range(N_HEADS_ATT)], axis=0)


def _head_tile(res, h):
    return res[8 * h:8 * h + 8, h * ATT_HW:(h + 1) * ATT_HW]


def _scores_kernel(pt_ref, q_ref, *refs, n_pages_step):
    k_refs = refs[:n_pages_step]
    s_ref = refs[n_pages_step]
    b = pl.program_id(0)
    q48 = _q_rows(q_ref, b).astype(BF16)
    for i in range(n_pages_step):
        k = k_refs[i][...].reshape(N_HEADS_ATT * PAGE, ATT_HW).astype(BF16)
        res = _dot_nt(q48, k)
        for h in range(N_HEADS_ATT):
            s_ref[h, :, i * PAGE:(i + 1) * PAGE] = _head_tile(res, h)


def _page_specs(layer, n_pages_step):
    def pmap(i):
        return lambda b, c, pt: (layer, pt[b, c * n_pages_step + i], 0, 0, 0)
    return [pl.BlockSpec((None, None, N_HEADS_ATT, PAGE, ATT_HW), pmap(i)) for i in range(n_pages_step)]


def _attn_scores(q, cache_kt, page_table, layer, n_pages_step):
    bd, n_pages = page_table.shape
    steps = n_pages // n_pages_step
    grid_spec = pltpu.PrefetchScalarGridSpec(
        num_scalar_prefetch=1,
        grid=(bd, steps),
        in_specs=[pl.BlockSpec((N_HEADS_ATT, bd, ATT_HW), lambda b, c, pt: (0, 0, 0))]
        + _page_specs(layer, n_pages_step),
        out_specs=pl.BlockSpec((None, N_HEADS_ATT, 8, n_pages_step * PAGE),
                               lambda b, c, pt: (b, 0, 0, c)),
    )
    return pl.pallas_call(
        functools.partial(_scores_kernel, n_pages_step=n_pages_step),
        grid_spec=grid_spec,
        out_shape=jax.ShapeDtypeStruct((bd, N_HEADS_ATT, 8, n_pages * PAGE), F32),
        compiler_params=_cparams(("parallel", "arbitrary")),
        name="attn_scores",
    )(page_table, q, *([cache_kt] * n_pages_step))


def _attn_pv_kernel(pt_ref, s_ref, q_ref, kn_ref, vn_ref, al_ref, sub_ref, *refs,
                    n_pages_step, past, lam_init, slopes):
    v_refs = refs[:n_pages_step]
    o_ref = refs[n_pages_step]
    a_ref, anew_ref, acc_ref = refs[n_pages_step + 1:]
    b = pl.program_id(0)
    c = pl.program_id(1)

    @pl.when(c == 0)
    def _():
        lam = _lambda(al_ref, lam_init)
        dist = (past - lax.broadcasted_iota(jnp.int32, (1, past), 1)).astype(F32)
        for h in range(N_HEADS_ATT):
            s = s_ref[h] - slopes[h] * dist
            qk = _q_two_maps(q_ref[h, pl.ds(b, 1), :]) * kn_ref[h, pl.ds(b, 1), :]
            s_new = jnp.sum(qk, axis=-1, keepdims=True)
            m = jnp.maximum(jnp.max(s, axis=-1, keepdims=True), s_new)
            p = jnp.exp(s - m)
            p_new = jnp.exp(s_new - m)
            l = jnp.sum(p, axis=-1, keepdims=True) + p_new
            p = p / l
            p_new = p_new / l
            a_ref[h] = jnp.broadcast_to(p[0:1] - lam * p[1:2], (8, past))
            anew_ref[h] = jnp.broadcast_to(p_new[0:1] - lam * p_new[1:2], (8, ATT_HW))
        acc_ref[...] = jnp.zeros_like(acc_ref)

    res = None
    for i in range(n_pages_step):
        start = pl.multiple_of((c * n_pages_step + i) * PAGE, PAGE)
        a48 = a_ref[:, :, pl.ds(start, PAGE)].reshape(N_HEADS_ATT * 8, PAGE).astype(BF16)
        v = jnp.concatenate([v_refs[i][h] for h in range(N_HEADS_ATT)], axis=1).astype(BF16)
        d = _dot(a48, v)
        res = d if res is None else res + d
    for h in range(N_HEADS_ATT):
        acc_ref[h] += _head_tile(res, h)

    @pl.when(c == pl.num_programs(1) - 1)
    def _():
        for h in range(N_HEADS_ATT):
            o = acc_ref[h][0:1] + anew_ref[h][0:1] * vn_ref[h, pl.ds(b, 1), :]
            o_ref[:, h * ATT_HW:(h + 1) * ATT_HW] = _rms(o, sub_ref[...]) * (1.0 - lam_init)


def _attn_pv(scores, q, k_new, v_new, att_lambda, subln, cache_vt, page_table, layer,
             n_pages_step, lam_init):
    bd, n_pages = page_table.shape
    past = n_pages * PAGE
    steps = n_pages // n_pages_step
    const2 = lambda b, c, pt: (0, 0)
    const3 = lambda b, c, pt: (0, 0, 0)
    hm = pl.BlockSpec((N_HEADS_ATT, bd, ATT_HW), const3)
    grid_spec = pltpu.PrefetchScalarGridSpec(
        num_scalar_prefetch=1,
        grid=(bd, steps),
        in_specs=[
            pl.BlockSpec((None, N_HEADS_ATT, 8, past), lambda b, c, pt: (b, 0, 0, 0)),
            hm, hm, hm,
            pl.BlockSpec((4, ATT_DH), const2),
            pl.BlockSpec((1, ATT_HW), const2),
        ] + _page_specs(layer, n_pages_step),
        out_specs=pl.BlockSpec((None, 1, ATT_W), lambda b, c, pt: (b, 0, 0)),
        scratch_shapes=[pltpu.VMEM((N_HEADS_ATT, 8, past), F32),
                        pltpu.VMEM((N_HEADS_ATT, 8, ATT_HW), F32),
                        pltpu.VMEM((N_HEADS_ATT, 8, ATT_HW), F32)],
    )
    out = pl.pallas_call(
        functools.partial(_attn_pv_kernel, n_pages_step=n_pages_step, past=past,
                          lam_init=lam_init, slopes=[float(x) for x in _alibi_slopes(N_HEADS_ATT)]),
        grid_spec=grid_spec,
        out_shape=jax.ShapeDtypeStruct((bd, 1, ATT_W), F32),
        compiler_params=_cparams(("parallel", "arbitrary")),
        name="attn_pv",
    )(page_table, scores, q, k_new, v_new, att_lambda, subln, *([cache_vt] * n_pages_step))
    return out.reshape(bd, ATT_W)


def _mlstm_chunk_kernel(q_ref, k_ref, v_ref, o_ref, g_ref, bias_ref, norm_ref, c0_ref, n0_ref,
                        m0_ref, h_ref, c_ref, n_ref, m_ref, *, L):
    @pl.when(pl.program_id(1) == 0)
    def _():
        c_ref[...] = c0_ref[...]
        n_ref[...] = n0_ref[...]
        m_ref[...] = m0_ref[...]

    g = g_ref[...] + bias_ref[...]
    g_t = g.T
    r = lax.broadcasted_iota(jnp.int32, (L, L), 0)
    c = lax.broadcasted_iota(jnp.int32, (L, L), 1)
    causal = c <= r

    def lanes(x, n):
        reps = [x] * (n // 128) + ([x[:, :n % 128]] if n % 128 else [])
        return reps[0] if len(reps) == 1 else jnp.concatenate(reps, axis=1)

    for h in range(N_HEADS_ML):
        sl = slice(h * ML_DH, (h + 1) * ML_DH)
        ig_r = g_t[h:h + 1, :]
        lf_r = _log_sigmoid(g_t[N_HEADS_ML + h:N_HEADS_ML + h + 1, :])
        ig_c = jnp.broadcast_to(g[:, h:h + 1], (L, 128))
        lf_c = jnp.broadcast_to(_log_sigmoid(g[:, N_HEADS_ML + h:N_HEADS_ML + h + 1]), (L, 128))
        b_c = jnp.broadcast_to(jnp.sum(jnp.where(causal, lf_r, 0.0), axis=1, keepdims=True),
                               (L, 128))
        b_r = jnp.sum(jnp.where(r <= c, lanes(lf_c, L), 0.0), axis=0, keepdims=True)
        m_prev = m_ref[h:h + 1, :]
        d = jnp.where(causal, lanes(b_c, L) + (ig_r - b_r), -jnp.inf)
        inter = b_c + m_prev
        m_t = jnp.maximum(inter, jnp.max(d, axis=1, keepdims=True))
        w_intra = jnp.exp(d - lanes(m_t, L))
        w_state = jnp.exp(inter - m_t)

        q = q_ref[:, sl]
        k = k_ref[:, sl] * (ML_DH ** -0.5)
        v = v_ref[:, sl]
        qb, kb, vb = q.astype(BF16), k.astype(BF16), v.astype(BF16)
        cst = c_ref[h]
        nst = n_ref[h:h + 1, :]
        a = w_intra * _dot_nt(qb, kb)
        num = _dot(a.astype(BF16), vb) + lanes(w_state, ML_DH) * _dot_nt(qb, cst.astype(BF16))
        den = jnp.sum(a, axis=1, keepdims=True) + w_state * jnp.sum(q * nst, axis=1, keepdims=True)
        hh = num * lanes(1.0 / jnp.maximum(jnp.abs(den), jnp.exp(-m_t)), ML_DH)

        m_new = m_t[L - 1:L, :]
        b_last = b_c[L - 1:L, :]
        w_s = lanes(jnp.exp(b_last - b_c + ig_c - m_new), ML_DH)
        decay = lanes(jnp.exp(b_last + m_prev - m_new), ML_DH)
        c_ref[h] = decay * cst + _dot_tn((w_s * v).astype(BF16), kb)
        n_ref[h:h + 1, :] = decay * nst + jnp.sum(w_s * k, axis=0, keepdims=True)
        m_ref[h:h + 1, :] = m_new

        hn = _rms(hh, norm_ref[:, sl])
        h_ref[:, sl] = (_sigmoid(o_ref[:, sl]) * hn).astype(h_ref.dtype)


def _mlstm_prompt(proj, gates, b_if_pad, norm, c0, n0, m0, n_seq, seq, L):
    nc = seq // L
    blk = lambda off: pl.BlockSpec((L, ML_W), lambda b, c, off=off: (b * nc + c, off // ML_W))
    const2 = lambda b, c: (0, 0)
    return pl.pallas_call(
        functools.partial(_mlstm_chunk_kernel, L=L),
        grid=(n_seq, nc),
        in_specs=[
            blk(R_MQ), blk(R_MK), blk(R_MV), blk(R_MO),
            pl.BlockSpec((L, GATE_W), lambda b, c: (b * nc + c, 0)),
            pl.BlockSpec((1, GATE_W), const2),
            pl.BlockSpec((1, ML_W), const2),
            pl.BlockSpec((None, N_HEADS_ML, ML_DH, ML_DH), lambda b, c: (b, 0, 0, 0)),
            pl.BlockSpec((None, N_HEADS_ML, ML_DH), lambda b, c: (b, 0, 0)),
            pl.BlockSpec((None, N_HEADS_ML, 128), lambda b, c: (b, 0, 0)),
        ],
        out_specs=[
            pl.BlockSpec((L, ML_W), lambda b, c: (b * nc + c, 0)),
            pl.BlockSpec((None, N_HEADS_ML, ML_DH, ML_DH), lambda b, c: (b, 0, 0, 0)),
            pl.BlockSpec((None, N_HEADS_ML, ML_DH), lambda b, c: (b, 0, 0)),
            pl.BlockSpec((None, N_HEADS_ML, 128), lambda b, c: (b, 0, 0)),
        ],
        out_shape=[
            jax.ShapeDtypeStruct((n_seq * seq, ML_W), BF16),
            jax.ShapeDtypeStruct((n_seq, N_HEADS_ML, ML_DH, ML_DH), F32),
            jax.ShapeDtypeStruct((n_seq, N_HEADS_ML, ML_DH), F32),
            jax.ShapeDtypeStruct((n_seq, N_HEADS_ML, 128), F32),
        ],
        compiler_params=_cparams(("parallel", "arbitrary")),
        name="mlstm_prompt",
    )(proj, proj, proj, proj, gates, b_if_pad, norm, c0, n0, m0)


def _mlstm_step_kernel(q_ref, k_ref, v_ref, o_ref, g_ref, bias_ref, norm_ref, c0_ref, n0_ref,
                       m0_ref, h_ref, c_ref, n_ref, m_ref):
    b = pl.program_id(0)
    g = g_ref[pl.ds(b, 1), :] + bias_ref[...]
    r = lax.broadcasted_iota(jnp.int32, (ML_DH, ML_DH), 0)
    c = lax.broadcasted_iota(jnp.int32, (ML_DH, ML_DH), 1)
    eye = r == c
    for h in range(N_HEADS_ML):
        sl = slice(h * ML_DH, (h + 1) * ML_DH)
        ig = g[:, h:h + 1]
        lf = _log_sigmoid(g[:, N_HEADS_ML + h:N_HEADS_ML + h + 1])
        m_prev = m0_ref[h:h + 1, 0:1]
        inter = lf + m_prev
        m_t = jnp.maximum(inter, ig)
        w_i = jnp.exp(ig - m_t)
        w_state = jnp.exp(inter - m_t)

        q = q_ref[pl.ds(b, 1), sl]
        k = k_ref[pl.ds(b, 1), sl] * (ML_DH ** -0.5)
        v = v_ref[pl.ds(b, 1), sl]
        cst = c0_ref[h]
        nst = n0_ref[h:h + 1, :]
        a = w_i * jnp.sum(q * k, axis=1, keepdims=True)
        cq = _dot_nt(jnp.broadcast_to(q, (8, ML_DH)).astype(BF16), cst.astype(BF16))[0:1]
        num = a * v + w_state * cq
        den = a + w_state * jnp.sum(q * nst, axis=1, keepdims=True)
        hh = num / jnp.maximum(jnp.abs(den), jnp.exp(-m_t))

        v_col = jnp.sum(jnp.where(eye, w_i * v, 0.0), axis=1, keepdims=True)
        c_ref[h] = w_state * cst + v_col * k
        n_ref[h:h + 1, :] = w_state * nst + w_i * k
        m_ref[h:h + 1, :] = jnp.broadcast_to(m_t, (1, 128))

        hn = _rms(hh, norm_ref[:, sl])
        h_ref[:, sl] = _sigmoid(o_ref[pl.ds(b, 1), sl]) * hn


def _mlstm_step(proj, gates, b_if_pad, norm, c0, n0, m0):
    bd = proj.shape[0]
    blk = lambda off: pl.BlockSpec((bd, ML_W), lambda b, off=off: (0, off // ML_W))
    const2 = lambda b: (0, 0)
    st4 = pl.BlockSpec((None, N_HEADS_ML, ML_DH, ML_DH), lambda b: (b, 0, 0, 0))
    st3 = pl.BlockSpec((None, N_HEADS_ML, ML_DH), lambda b: (b, 0, 0))
    stm = pl.BlockSpec((None, N_HEADS_ML, 128), lambda b: (b, 0, 0))
    out = pl.pallas_call(
        _mlstm_step_kernel,
        grid=(bd,),
        in_specs=[blk(R_MQ), blk(R_MK), blk(R_MV), blk(R_MO),
                  pl.BlockSpec((bd, GATE_W), const2),
                  pl.BlockSpec((1, GATE_W), const2),
                  pl.BlockSpec((1, ML_W), const2),
                  st4, st3, stm],
        out_specs=[pl.BlockSpec((None, 1, ML_W), lambda b: (b, 0, 0)), st4, st3, stm],
        out_shape=[
            jax.ShapeDtypeStruct((bd, 1, ML_W), F32),
            jax.ShapeDtypeStruct((bd, N_HEADS_ML, ML_DH, ML_DH), F32),
            jax.ShapeDtypeStruct((bd, N_HEADS_ML, ML_DH), F32),
            jax.ShapeDtypeStruct((bd, N_HEADS_ML, 128), F32),
        ],
        compiler_params=_cparams(("parallel",)),
        name="mlstm_step",
    )(proj, proj, proj, proj, gates, b_if_pad, norm, c0, n0, m0)
    return (out[0].reshape(bd, ML_W),) + tuple(out[1:])


def _conv3_rows(u, w_ref, prev8):
    w0, w1, w2 = w_ref[0:1, :], w_ref[1:2, :], w_ref[2:3, :]
    y = w0 * pltpu.roll(u, 2, 0) + w1 * pltpu.roll(u, 1, 0) + w2 * u
    head = u[0:8, :]
    row = lax.broadcasted_iota(jnp.int32, head.shape, 0)
    p1 = prev8[7:8, :]
    p2 = prev8[6:7, :]
    h1 = jnp.where(row == 0, p1, pltpu.roll(head, 1, 0))
    h2 = jnp.where(row == 0, p2, jnp.where(row == 1, p1, pltpu.roll(head, 2, 0)))
    return jnp.concatenate([w0 * h2 + w1 * h1 + w2 * head, y[8:, :]], axis=0)


def _conv3_state(u, w_ref, s0, s1):
    return w_ref[0:1, :] * s0 + w_ref[1:2, :] * s1 + w_ref[2:3, :] * u


def _outproj_prompt_kernel(x_ref, a_ref, hm_ref, cb_ref, cc_ref, cx_ref, cw_ref, w_ref,
                           o_ref, st_ref, carry_ref, *, tiles_per_seq):
    i = pl.program_id(0)
    u = cc_ref[...] * cx_ref[...]
    prev8 = jnp.where(i % tiles_per_seq == 0, 0.0, carry_ref[...])
    yc = cb_ref[...] * _conv3_rows(u, cw_ref, prev8)
    tail = u[u.shape[0] - 8:, :]
    carry_ref[...] = tail
    st_ref[...] = tail
    acc = _dot(a_ref[...], w_ref[0:ATT_W, :])
    acc = acc + _dot(hm_ref[...], w_ref[ATT_W:ATT_W + ML_W, :])
    acc = acc + _dot(yc.astype(BF16), w_ref[ATT_W + ML_W:, :])
    o_ref[...] = x_ref[...] + acc


def _outproj_prompt(x, a, hm, proj, conv_w, w_out, wl, n_seq, seq, tm):
    m = x.shape[0]
    tps = seq // tm
    cblk = lambda off: pl.BlockSpec((tm, CONV_C), lambda i, off=off: (i, off // CONV_C))
    return pl.pallas_call(
        functools.partial(_outproj_prompt_kernel, tiles_per_seq=tps),
        grid=(m // tm,),
        in_specs=[
            pl.BlockSpec((tm, D_MODEL), lambda i: (i, 0)),
            pl.BlockSpec((tm, ATT_W), lambda i: (i, 0)),
            pl.BlockSpec((tm, ML_W), lambda i: (i, 0)),
            cblk(R_CB), cblk(R_CC), cblk(R_CX),
            pl.BlockSpec((3, CONV_C), lambda i: (0, 0)),
            pl.BlockSpec((None, D_MODEL, D_MODEL), lambda i: (wl, 0, 0)),
        ],
        out_specs=[
            pl.BlockSpec((tm, D_MODEL), lambda i: (i, 0)),
            pl.BlockSpec((None, 8, CONV_C), lambda i: (i, 0, 0)),
        ],
        out_shape=[
            jax.ShapeDtypeStruct((m, D_MODEL), F32),
            jax.ShapeDtypeStruct((m // tm, 8, CONV_C), F32),
        ],
        scratch_shapes=[pltpu.VMEM((8, CONV_C), F32)],
        compiler_params=_cparams(("arbitrary",)),
        name="outproj_prompt",
    )(x, a, hm, proj, proj, proj, conv_w, w_out)


def _outproj_step_kernel(x_ref, a_ref, hm_ref, cb_ref, cc_ref, cx_ref, cw_ref, s0_ref, s1_ref,
                         w_ref, o_ref, u_ref):
    u = cc_ref[...] * cx_ref[...]
    u_ref[...] = u
    yc = cb_ref[...] * _conv3_state(u, cw_ref, s0_ref[...], s1_ref[...])
    acc = _dot(a_ref[...].astype(BF16), w_ref[0:ATT_W, :])
    acc = acc + _dot(hm_ref[...].astype(BF16), w_ref[ATT_W:ATT_W + ML_W, :])
    acc = acc + _dot(yc.astype(BF16), w_ref[ATT_W + ML_W:, :])
    o_ref[...] = x_ref[...] + acc


def _outproj_step(x, a, hm, proj, conv_w, s0, s1, w_out, wl):
    bd = x.shape[0]
    full = lambda shp: pl.BlockSpec(shp, lambda i: (0, 0))
    cblk = lambda off: pl.BlockSpec((bd, CONV_C), lambda i, off=off: (0, off // CONV_C))
    return pl.pallas_call(
        _outproj_step_kernel,
        grid=(1,),
        in_specs=[full((bd, D_MODEL)), full((bd, ATT_W)), full((bd, ML_W)),
                  cblk(R_CB), cblk(R_CC), cblk(R_CX),
                  full((3, CONV_C)), full((bd, CONV_C)), full((bd, CONV_C)),
                  pl.BlockSpec((None, D_MODEL, D_MODEL), lambda i: (wl, 0, 0))],
        out_specs=[full((bd, D_MODEL)), full((bd, CONV_C))],
        out_shape=[jax.ShapeDtypeStruct((bd, D_MODEL), F32),
                   jax.ShapeDtypeStruct((bd, CONV_C), F32)],
        compiler_params=_cparams(("arbitrary",)),
        name="outproj_step",
    )(x, a, hm, proj, proj, proj, conv_w, s0, s1, w_out)


def _ffn_prompt_kernel(x_ref, g_ref, wg_ref, wv_ref, cwg_ref, cwv_ref, wd_ref, gfin_ref,
                       o_ref, stg_ref, stv_ref, h_ref, carry_ref, *, tiles_per_seq, final_norm):
    i = pl.program_id(0)
    j = pl.program_id(1)

    @pl.when(j == 0)
    def _():
        x = x_ref[...]
        h_ref[...] = _rms(x, g_ref[...]).astype(BF16)
        o_ref[...] = x

    h = h_ref[...]
    ug = _dot(h, wg_ref[...])
    uv = _dot(h, wv_ref[...])
    start = i % tiles_per_seq == 0
    pg = jnp.where(start, 0.0, carry_ref[j, 0])
    pv = jnp.where(start, 0.0, carry_ref[j, 1])
    cg = _conv3_rows(ug, cwg_ref, pg)
    cv = _conv3_rows(uv, cwv_ref, pv)
    tm = ug.shape[0]
    tg = ug[tm - 8:, :]
    tv = uv[tm - 8:, :]
    carry_ref[j, 0] = tg
    carry_ref[j, 1] = tv
    stg_ref[...] = tg
    stv_ref[...] = tv
    act = (cg * _sigmoid(cg) * cv).astype(BF16)
    o_ref[...] += _dot(act, wd_ref[...])
    if final_norm:
        @pl.when(j == pl.num_programs(1) - 1)
        def _():
            o_ref[...] = _rms(o_ref[...], gfin_ref[...])


def _ffn_prompt(x, g, w_up, ffn_conv_w, w_down, wl, g_final, final_norm, n_seq, seq, tm, tf):
    m = x.shape[0]
    tps = seq // tm
    nj = D_FF // tf
    return pl.pallas_call(
        functools.partial(_ffn_prompt_kernel, tiles_per_seq=tps, final_norm=final_norm),
        grid=(m // tm, nj),
        in_specs=[
            pl.BlockSpec((tm, D_MODEL), lambda i, j: (i, 0)),
            pl.BlockSpec((1, D_MODEL), lambda i, j: (0, 0)),
            pl.BlockSpec((None, D_MODEL, tf), lambda i, j: (wl, 0, j)),
            pl.BlockSpec((None, D_MODEL, tf), lambda i, j: (wl, 0, nj + j)),
            pl.BlockSpec((3, tf), lambda i, j: (0, j)),
            pl.BlockSpec((3, tf), lambda i, j: (0, nj + j)),
            pl.BlockSpec((None, tf, D_MODEL), lambda i, j: (wl, j, 0)),
            pl.BlockSpec((1, D_MODEL), lambda i, j: (0, 0)),
        ],
        out_specs=[
            pl.BlockSpec((tm, D_MODEL), lambda i, j: (i, 0)),
            pl.BlockSpec((None, 8, tf), lambda i, j: (i, 0, j)),
            pl.BlockSpec((None, 8, tf), lambda i, j: (i, 0, j)),
        ],
        out_shape=[
            jax.ShapeDtypeStruct((m, D_MODEL), F32),
            jax.ShapeDtypeStruct((m // tm, 8, D_FF), F32),
            jax.ShapeDtypeStruct((m // tm, 8, D_FF), F32),
        ],
        scratch_shapes=[pltpu.VMEM((tm, D_MODEL), BF16), pltpu.VMEM((nj, 2, 8, tf), F32)],
        compiler_params=_cparams(("arbitrary", "arbitrary")),
        name="ffn_prompt",
    )(x, g, w_up, w_up, ffn_conv_w, ffn_conv_w, w_down, g_final)


def _ffn_step_kernel(x_ref, g_ref, wg_ref, wv_ref, cwg_ref, cwv_ref, s0g_ref, s1g_ref, s0v_ref,
                     s1v_ref, wd_ref, gfin_ref, o_ref, ug_ref, uv_ref, h_ref, *, final_norm):
    @pl.when(pl.program_id(0) == 0)
    def _():
        x = x_ref[...]
        h_ref[...] = _rms(x, g_ref[...]).astype(BF16)
        o_ref[...] = x

    h = h_ref[...]
    ug = _dot(h, wg_ref[...])
    uv = _dot(h, wv_ref[...])
    ug_ref[...] = ug
    uv_ref[...] = uv
    cg = _conv3_state(ug, cwg_ref, s0g_ref[...], s1g_ref[...])
    cv = _conv3_state(uv, cwv_ref, s0v_ref[...], s1v_ref[...])
    act = (cg * _sigmoid(cg) * cv).astype(BF16)
    o_ref[...] += _dot(act, wd_ref[...])
    if final_norm:
        @pl.when(pl.program_id(0) == pl.num_programs(0) - 1)
        def _():
            o_ref[...] = _rms(o_ref[...], gfin_ref[...])


def _ffn_step(x, g, w_up, ffn_conv_w, s0, s1, w_down, wl, g_final, final_norm, tf):
    bd = x.shape[0]
    nj = D_FF // tf
    gcol = lambda rows: pl.BlockSpec((rows, tf), lambda j: (0, j))
    vcol = lambda rows: pl.BlockSpec((rows, tf), lambda j: (0, nj + j))
    return pl.pallas_call(
        functools.partial(_ffn_step_kernel, final_norm=final_norm),
        grid=(nj,),
        in_specs=[
            pl.BlockSpec((bd, D_MODEL), lambda j: (0, 0)),
            pl.BlockSpec((1, D_MODEL), lambda j: (0, 0)),
            pl.BlockSpec((None, D_MODEL, tf), lambda j: (wl, 0, j)),
            pl.BlockSpec((None, D_MODEL, tf), lambda j: (wl, 0, nj + j)),
            gcol(3), vcol(3),
            gcol(bd), gcol(bd), vcol(bd), vcol(bd),
            pl.BlockSpec((None, tf, D_MODEL), lambda j: (wl, j, 0)),
            pl.BlockSpec((1, D_MODEL), lambda j: (0, 0)),
        ],
        out_specs=[pl.BlockSpec((bd, D_MODEL), lambda j: (0, 0)), gcol(bd), gcol(bd)],
        out_shape=[jax.ShapeDtypeStruct((bd, D_MODEL), F32),
                   jax.ShapeDtypeStruct((bd, D_FF), F32),
                   jax.ShapeDtypeStruct((bd, D_FF), F32)],
        scratch_shapes=[pltpu.VMEM((bd, D_MODEL), BF16)],
        compiler_params=_cparams(("arbitrary",)),
        name="ffn_step",
    )(x, g, w_up, w_up, ffn_conv_w, ffn_conv_w, s0, s1, s0, s1, w_down, g_final)


def _tiles(seq):
    tm = min(512, seq)
    tmi = min(1024, seq)
    tf = 512
    tq = min(256, seq)
    rs = tq
    lc = min(512, seq)
    return tm, tmi, tf, tq, rs, lc


def kernel(x_prompt, x_sample, cache_k, cache_v, state_mlstm_c, state_mlstm_n, state_mlstm_m,
           state_conv, state_ffn_conv, page_table,
           g_mix, w_in, b_if, att_lambda, att_subln, mlstm_norm, conv_w, w_out,
           g_ffn, w_up, ffn_conv_w, w_down, g_final):
    depth = w_in.shape[0]
    n_seq, seq, _ = x_prompt.shape
    bd = x_sample.shape[0]
    n_pool = cache_k.shape[1]
    tm, tmi, tf, tq, rs, lc = _tiles(seq)
    tps = seq // tm
    n_pages = page_table.shape[1]
    pages_step = math.gcd(32, n_pages)

    n_gate = 2 * N_HEADS_ML
    w_bf = w_in.transpose(0, 2, 1).astype(BF16)
    w_conv = w_bf[:, N_QKV_MLSTM + n_gate:, :]
    w_out_b = w_out.astype(BF16)
    w_up_b = w_up.astype(BF16)
    w_down_b = w_down.astype(BF16)
    b_if_pad = jnp.pad(b_if, ((0, 0), (0, GATE_W - n_gate))).reshape(depth, 1, GATE_W)

    slopes = jnp.asarray(_alibi_slopes(N_HEADS_ATT))
    cache_kt = cache_k.transpose(0, 1, 3, 2, 4)
    cache_vt = cache_v.transpose(0, 1, 3, 2, 4)

    c0p = jnp.zeros((n_seq, N_HEADS_ML, ML_DH, ML_DH), F32)
    n0p = jnp.zeros((n_seq, N_HEADS_ML, ML_DH), F32)
    m0p = jnp.full((n_seq, N_HEADS_ML, 128), -jnp.inf, F32)

    xp = x_prompt.reshape(n_seq * seq, D_MODEL)
    xs = x_sample.reshape(bd, D_MODEL)
    outs_p = [[] for _ in range(7)]
    outs_s = [[] for _ in range(7)]
    k_all = v_all = None
    gfin = g_final.reshape(1, D_MODEL)
    for l in range(depth):
        last = l == depth - 1
        lam_init = 0.8 - 0.6 * math.exp(-0.3 * l)
        gm = g_mix[l].reshape(1, D_MODEL)
        gf = g_ffn[l].reshape(1, D_MODEL)
        sub = att_subln[l].reshape(1, ATT_HW)
        mnorm = mlstm_norm[l].reshape(1, ML_W)

        proj, gates, q, k_all, v_all = _inproj(xp, gm, w_bf, w_conv, l, n_seq, seq, tmi, l, depth,
                                               k_all, v_all)
        a = _attn_prompt(q, k_all, v_all, l, slopes, att_lambda[l], sub, n_seq, seq, tq, rs, lam_init)
        hm, c1, n1, m1 = _mlstm_prompt(proj, gates, b_if_pad[l], mnorm, c0p, n0p, m0p, n_seq, seq, lc)
        xp, cst = _outproj_prompt(xp, a, hm, proj, conv_w[l], w_out_b, l, n_seq, seq, tm)
        xp, stg, stv = _ffn_prompt(xp, gf, w_up_b, ffn_conv_w[l], w_down_b, l, gfin, last,
                                   n_seq, seq, tm, tf)
        outs_p[2].append(c1)
        outs_p[3].append(n1)
        outs_p[4].append(m1[:, :, 0])
        outs_p[5].append(cst[tps - 1::tps, 6:8, :])
        outs_p[6].append(jnp.concatenate([stg[tps - 1::tps, 6:8, :], stv[tps - 1::tps, 6:8, :]], axis=-1))

        proj_s, gates_s, q_s, k_s, v_s = _inproj(xs, gm, w_bf, w_conv, l, 1, bd, bd, 0, 1)
        q_s, k_s, v_s = q_s[0], k_s[0, 0], v_s[0, 0]
        scores = _attn_scores(q_s, cache_kt, page_table, l, pages_step)
        a_s = _attn_pv(scores, q_s, k_s, v_s, att_lambda[l], sub, cache_vt, page_table, l,
                       pages_step, lam_init)
        m0s = jnp.broadcast_to(state_mlstm_m[l][:, :, None], (bd, N_HEADS_ML, 128))
        hm_s, c1s, n1s, m1s = _mlstm_step(proj_s, gates_s, b_if_pad[l], mnorm, state_mlstm_c[l],
                                          state_mlstm_n[l], m0s)
        sc = state_conv[l]
        xs, u_s = _outproj_step(xs, a_s, hm_s, proj_s, conv_w[l], sc[:, 0], sc[:, 1], w_out_b, l)
        sf = state_ffn_conv[l]
        xs, ug_s, uv_s = _ffn_step(xs, gf, w_up_b, ffn_conv_w[l], sf[:, 0], sf[:, 1], w_down_b, l,
                                   gfin, last, tf)
        outs_s[0].append(k_s.transpose(1, 0, 2).reshape(bd, 1, N_HEADS_ATT, ATT_HW))
        outs_s[1].append(v_s.transpose(1, 0, 2).reshape(bd, 1, N_HEADS_ATT, ATT_HW))
        outs_s[2].append(c1s)
        outs_s[3].append(n1s)
        outs_s[4].append(m1s[:, :, 0])
        outs_s[5].append(jnp.stack([sc[:, 1], u_s], axis=1))
        outs_s[6].append(jnp.stack([sf[:, 1], jnp.concatenate([ug_s, uv_s], axis=-1)], axis=1))

    y_prompt = xp.reshape(n_seq, seq, D_MODEL)
    y_sample = xs.reshape(bd, 1, D_MODEL)
    c_p, n_p, m_p, conv_p, ffn_p = [jnp.stack(o, 0) for o in outs_p[2:]]
    k_p = k_all.transpose(0, 1, 3, 2, 4)
    v_p = v_all.transpose(0, 1, 3, 2, 4)
    k_s_, v_s_, c_s, n_s, m_s, conv_s, ffn_s = [jnp.stack(o, 0) for o in outs_s]
    return (y_prompt, y_sample, k_p, v_p, k_s_, v_s_, c_p, n_p, m_p, c_s, n_s, m_s,
            conv_p, conv_s, ffn_p, ffn_s)
```

```python
import functools
import math

import numpy as np
import jax
import jax.numpy as jnp
from jax import lax
from jax.experimental import pallas as pl
from jax.experimental.pallas import tpu as pltpu

F32 = jnp.float32
BF16 = jnp.bfloat16

D_MODEL = 2048
N_HEADS_ATT = 6
ATT_DH = 64
ATT_HW = 2 * ATT_DH
ATT_W = N_HEADS_ATT * ATT_HW
N_HEADS_ML = 4
ML_DH = 192
ML_W = N_HEADS_ML * ML_DH
CONV_C = D_MODEL - ATT_W - ML_W
D_FF = 5632
PAGE = 128
EPS = 1e-6
LOG2E = 1.4426950408889634

GATE_W = 128

VMEM_LIMIT = 56 * 1024 * 1024


def _cparams(sem):
    return pltpu.CompilerParams(dimension_semantics=sem, vmem_limit_bytes=VMEM_LIMIT)


def _alibi_slopes(n_heads):
    def geometric(n):
        start = 2.0 ** (-8.0 / n)
        return [start ** (i + 1) for i in range(n)]
    closest = 2 ** int(math.floor(math.log2(n_heads)))
    slopes = geometric(closest)
    if closest < n_heads:
        slopes = slopes + geometric(2 * closest)[0::2][: n_heads - closest]
    return np.asarray(slopes, dtype=np.float32)


def _rms(x, g):
    return x * lax.rsqrt(jnp.mean(x * x, axis=-1, keepdims=True) + EPS) * g


def _log_sigmoid(x):
    return jnp.minimum(x, 0.0) - jnp.log1p(jnp.exp(-jnp.abs(x)))


def _sigmoid(x):
    return 0.5 * (jnp.tanh(0.5 * x) + 1.0)


def _lambda(al_ref, lam_init):
    al = al_ref[...]
    s01 = jnp.sum(al[0:1] * al[1:2], axis=-1, keepdims=True)
    s23 = jnp.sum(al[2:3] * al[3:4], axis=-1, keepdims=True)
    return jnp.exp(s01) - jnp.exp(s23) + lam_init


def _dot_nt(a, b):
    return lax.dot_general(a, b, (((1,), (1,)), ((), ())), preferred_element_type=F32)


def _dot_tn(a, b):
    return lax.dot_general(a, b, (((0,), (0,)), ((), ())), preferred_element_type=F32)


def _dot(a, b):
    return jnp.dot(a, b, preferred_element_type=F32)


TN = ATT_W
N_QKV_MLSTM = 3 * ATT_W + 4 * ML_W
J_ML = 3
J_CONV = N_QKV_MLSTM // TN
N_REST = 4 * ML_W + 3 * CONV_C
R_MQ, R_MK, R_MV, R_MO = 0, ML_W, 2 * ML_W, 3 * ML_W
R_CB = 4 * ML_W
R_CC, R_CX = R_CB + CONV_C, R_CB + 2 * CONV_C


def _inproj_kernel(*refs, layer_slot):
    if layer_slot:
        refs = refs[2:]
    x_ref, g_ref, w_ref, wc_ref, wg_ref, rest_ref, og_ref, q_ref, k_ref, v_ref, h_ref = refs
    j = pl.program_id(2)

    @pl.when(j == 0)
    def _():
        h = _rms(x_ref[...], g_ref[...]).astype(BF16)
        h_ref[...] = h
        og_ref[...] = _dot_nt(h, wg_ref[...])

    for t, ref in enumerate((q_ref, k_ref, v_ref)):
        @pl.when(j == t)
        def _(ref=ref):
            res = _dot_nt(h_ref[...], w_ref[...])
            for h in range(N_HEADS_ATT):
                ref[h] = res[:, h * ATT_HW:(h + 1) * ATT_HW]

    @pl.when((j >= J_ML) & (j < J_CONV))
    def _():
        rest_ref[...] = _dot_nt(h_ref[...], w_ref[...])

    @pl.when(j >= J_CONV)
    def _():
        rest_ref[...] = _dot_nt(h_ref[...], wc_ref[...])


def _inproj(x, g, w_bf, w_conv, wl, n_seq, seq, tm, layer, depth, k_all=None, v_all=None):
    m = x.shape[0]
    tps = seq // tm
    nj = J_CONV + 3 * CONV_C // TN
    alias = k_all is not None
    rest_map = lambda b, i, j: (b * tps + i, jnp.maximum(j - J_ML, 0))
    row = lambda b, i, j: (b * tps + i, 0)
    const = lambda b, i, j: (0, 0)
    once = pl.Buffered(1)
    kv_spec = pl.BlockSpec((None, None, N_HEADS_ATT, tm, ATT_HW), lambda b, i, j: (layer, b, 0, i, 0),
                           pipeline_mode=once)
    in_specs = [
        pl.BlockSpec((tm, D_MODEL), row),
        pl.BlockSpec((1, D_MODEL), const),
        pl.BlockSpec((None, TN, D_MODEL), lambda b, i, j: (wl, jnp.minimum(j, J_CONV - 1), 0)),
        pl.BlockSpec((None, TN, D_MODEL), lambda b, i, j: (wl, jnp.maximum(j - J_CONV, 0), 0)),
        pl.BlockSpec((None, GATE_W, D_MODEL), lambda b, i, j: (wl, N_QKV_MLSTM // GATE_W, 0)),
    ]
    args = [x, g, w_bf, w_conv, w_bf]
    if alias:
        in_specs = [pl.BlockSpec(memory_space=pl.ANY), pl.BlockSpec(memory_space=pl.ANY)] + in_specs
        args = [k_all, v_all] + args
    kv_shape = jax.ShapeDtypeStruct((depth, n_seq, N_HEADS_ATT, seq, ATT_HW), F32)
    return pl.pallas_call(
        functools.partial(_inproj_kernel, layer_slot=alias),
        grid=(n_seq, tps, nj),
        in_specs=in_specs,
        out_specs=[
            pl.BlockSpec((tm, TN), rest_map),
            pl.BlockSpec((tm, GATE_W), row),
            pl.BlockSpec((None, N_HEADS_ATT, tm, ATT_HW), lambda b, i, j: (b, 0, i, 0), pipeline_mode=once),
            kv_spec, kv_spec,
        ],
        out_shape=[
            jax.ShapeDtypeStruct((m, N_REST), F32),
            jax.ShapeDtypeStruct((m, GATE_W), F32),
            jax.ShapeDtypeStruct((n_seq, N_HEADS_ATT, seq, ATT_HW), F32),
            kv_shape, kv_shape,
        ],
        scratch_shapes=[pltpu.VMEM((tm, D_MODEL), BF16)],
        input_output_aliases={0: 3, 1: 4} if alias else {},
        compiler_params=_cparams(("parallel", "parallel", "arbitrary")),
        name="inproj",
    )(*args)


def _attn_prompt_kernel(slopes_ref, qa_ref, qb_ref, k_ref, v_ref, al_ref, sub_ref, oa_ref, ob_ref,
                        q2_ref, kb_ref, vb_ref, m_ref, l_ref, acc_ref, *, tq, rs, nq, lam_init):
    h = pl.program_id(1)
    i = pl.program_id(2)
    slope = slopes_ref[h]
    tile_a, tile_b = i, nq - 1 - i

    @pl.when(i == 0)
    def _():
        kb_ref[...] = k_ref[...].astype(BF16)
        vb_ref[...] = v_ref[...].astype(BF16)

    lane = lax.broadcasted_iota(jnp.int32, (tq, ATT_HW), 1)
    for n, ref in enumerate((qa_ref, qb_ref)):
        q = ref[...] * (ATT_DH ** -0.5 * LOG2E)
        q2_ref[2 * n * tq:(2 * n + 1) * tq, :] = jnp.where(lane < ATT_DH, q, 0.0).astype(BF16)
        q2_ref[(2 * n + 1) * tq:(2 * n + 2) * tq, :] = jnp.where(lane >= ATT_DH, q, 0.0).astype(BF16)
    m_ref[...] = jnp.full_like(m_ref, -jnp.inf)
    l_ref[...] = jnp.zeros_like(l_ref)
    acc_ref[...] = jnp.zeros_like(acc_ref)
    col = lax.broadcasted_iota(jnp.int32, (1, tq), 1)

    def block(row_base, tile, kblk, diag):
        start = pl.multiple_of(kblk * tq, tq)
        bias = (slope * LOG2E) * (col + (kblk - tile) * tq).astype(F32)
        for t in range(2 * tq // rs):
            r0 = row_base + t * rs
            rows = pl.ds(r0 if isinstance(r0, int) else pl.multiple_of(r0, rs), rs)
            q0 = (t * rs) % tq
            nk = min(tq, -(-(q0 + rs) // ATT_HW) * ATT_HW) if diag else tq
            k = kb_ref[pl.ds(start, nk), :]
            v = vb_ref[pl.ds(start, nk), :]
            s = _dot_nt(q2_ref[rows, :], k) + bias[:, :nk]
            if diag:
                r = lax.broadcasted_iota(jnp.int32, (rs, nk), 0) + q0
                c = lax.broadcasted_iota(jnp.int32, (rs, nk), 1)
                s = jnp.where(c <= r, s, -jnp.inf)
            m_old = m_ref[rows, :]
            m_new = jnp.maximum(m_old, jnp.max(s, axis=-1, keepdims=True))
            alpha = jnp.exp2(m_old - m_new)
            p = jnp.exp2(s - jnp.tile(m_new, (1, nk // ATT_HW)))
            l_ref[rows, :] = alpha * l_ref[rows, :] + jnp.sum(p, axis=-1, keepdims=True)
            acc_ref[rows, :] = alpha * acc_ref[rows, :] + _dot(p.astype(BF16), v)
            m_ref[rows, :] = m_new

    for u in range(nq - 1):
        to_b = (u >= i).astype(jnp.int32)
        block(to_b * (2 * tq), jnp.where(u >= i, tile_b, tile_a), u - to_b * i, False)
    block(0, tile_a, tile_a, True)
    block(2 * tq, tile_b, tile_b, True)

    lam = _lambda(al_ref, lam_init)
    o = acc_ref[...] / l_ref[...]
    for n, ref in enumerate((oa_ref, ob_ref)):
        on = o[2 * n * tq:(2 * n + 1) * tq] - lam * o[(2 * n + 1) * tq:(2 * n + 2) * tq]
        ref[...] = (_rms(on, sub_ref[...]) * (1.0 - lam_init)).astype(ref.dtype)


def _attn_prompt(q, k_all, v_all, layer, slopes, att_lambda, subln, n_seq, seq, tq, rs, lam_init):
    nq = seq // tq
    half = nq // 2
    w = ATT_HW
    kv_spec = pl.BlockSpec((None, None, None, seq, w), lambda b, h, i: (layer, b, h, 0, 0))
    o_shape = jax.ShapeDtypeStruct((n_seq, half, tq, ATT_W), BF16)
    lo, hi = pl.pallas_call(
        functools.partial(_attn_prompt_kernel, tq=tq, rs=rs, nq=nq, lam_init=lam_init),
        grid=(n_seq, N_HEADS_ATT, half),
        in_specs=[
            pl.BlockSpec(memory_space=pltpu.SMEM),
            pl.BlockSpec((None, None, tq, w), lambda b, h, i: (b, h, i, 0)),
            pl.BlockSpec((None, None, tq, w), lambda b, h, i: (b, h, nq - 1 - i, 0)),
            kv_spec, kv_spec,
            pl.BlockSpec((4, ATT_DH), lambda b, h, i: (0, 0)),
            pl.BlockSpec((1, w), lambda b, h, i: (0, 0)),
        ],
        out_specs=[pl.BlockSpec((None, None, tq, w), lambda b, h, i: (b, i, 0, h)),
                   pl.BlockSpec((None, None, tq, w), lambda b, h, i: (b, half - 1 - i, 0, h))],
        out_shape=[o_shape, o_shape],
        scratch_shapes=[pltpu.VMEM((4 * tq, w), BF16), pltpu.VMEM((seq, w), BF16),
                        pltpu.VMEM((seq, w), BF16), pltpu.VMEM((4 * tq, w), F32),
                        pltpu.VMEM((4 * tq, w), F32), pltpu.VMEM((4 * tq, w), F32)],
        compiler_params=_cparams(("parallel", "parallel", "arbitrary")),
        name="attn_prompt",
    )(slopes, q, q, k_all, v_all, att_lambda, subln)
    return jnp.concatenate([lo, hi], axis=1).reshape(n_seq * seq, ATT_W)


def _q_two_maps(q_row):
    r = lax.broadcasted_iota(jnp.int32, (8, ATT_HW), 0)
    c = lax.broadcasted_iota(jnp.int32, (8, ATT_HW), 1)
    keep = ((r == 0) & (c < ATT_DH)) | ((r == 1) & (c >= ATT_DH))
    return jnp.where(keep, q_row * (ATT_DH ** -0.5), 0.0)


def _q_rows(q_ref, b):
    return jnp.concatenate([_q_two_maps(q_ref[h, pl.ds(b, 1), :]) for h in range(N_HEADS_ATT)], axis=0)


def _head_tile(res, h):
    return res[8 * h:8 * h + 8, h * ATT_HW:(h + 1) * ATT_HW]


def _scores_kernel(pt_ref, q_ref, *refs, n_pages_step):
    k_refs = refs[:n_pages_step]
    s_ref = refs[n_pages_step]
    b = pl.program_id(0)
    q48 = _q_rows(q_ref, b).astype(BF16)
    for i in range(n_pages_step):
        k = k_refs[i][...].reshape(N_HEADS_ATT * PAGE, ATT_HW).astype(BF16)
        res = _dot_nt(q48, k)
        for h in range(N_HEADS_ATT):
            s_ref[h, :, i * PAGE:(i + 1) * PAGE] = _head_tile(res, h)


def _page_specs(layer, n_pages_step):
    def pmap(i):
        return lambda b, c, pt: (layer, pt[b, c * n_pages_step + i], 0, 0, 0)
    return [pl.BlockSpec((None, None, N_HEADS_ATT, PAGE, ATT_HW), pmap(i)) for i in range(n_pages_step)]


def _attn_scores(q, cache_kt, page_table, layer, n_pages_step):
    bd, n_pages = page_table.shape
    steps = n_pages // n_pages_step
    grid_spec = pltpu.PrefetchScalarGridSpec(
        num_scalar_prefetch=1,
        grid=(bd, steps),
        in_specs=[pl.BlockSpec((N_HEADS_ATT, bd, ATT_HW), lambda b, c, pt: (0, 0, 0))]
        + _page_specs(layer, n_pages_step),
        out_specs=pl.BlockSpec((None, N_HEADS_ATT, 8, n_pages_step * PAGE),
                               lambda b, c, pt: (b, 0, 0, c)),
    )
    return pl.pallas_call(
        functools.partial(_scores_kernel, n_pages_step=n_pages_step),
        grid_spec=grid_spec,
        out_shape=jax.ShapeDtypeStruct((bd, N_HEADS_ATT, 8, n_pages * PAGE), F32),
        compiler_params=_cparams(("parallel", "arbitrary")),
        name="attn_scores",
    )(page_table, q, *([cache_kt] * n_pages_step))


def _attn_pv_kernel(pt_ref, s_ref, q_ref, kn_ref, vn_ref, al_ref, sub_ref, *refs,
                    n_pages_step, past, lam_init, slopes):
    v_refs = refs[:n_pages_step]
    o_ref = refs[n_pages_step]
    a_ref, anew_ref, acc_ref = refs[n_pages_step + 1:]
    b = pl.program_id(0)
    c = pl.program_id(1)

    @pl.when(c == 0)
    def _():
        lam = _lambda(al_ref, lam_init)
        dist = (past - lax.broadcasted_iota(jnp.int32, (1, past), 1)).astype(F32)
        for h in range(N_HEADS_ATT):
            s = s_ref[h] - slopes[h] * dist
            qk = _q_two_maps(q_ref[h, pl.ds(b, 1), :]) * kn_ref[h, pl.ds(b, 1), :]
            s_new = jnp.sum(qk, axis=-1, keepdims=True)
            m = jnp.maximum(jnp.max(s, axis=-1, keepdims=True), s_new)
            p = jnp.exp(s - m)
            p_new = jnp.exp(s_new - m)
            l = jnp.sum(p, axis=-1, keepdims=True) + p_new
            p = p / l
            p_new = p_new / l
            a_ref[h] = jnp.broadcast_to(p[0:1] - lam * p[1:2], (8, past))
            anew_ref[h] = jnp.broadcast_to(p_new[0:1] - lam * p_new[1:2], (8, ATT_HW))
        acc_ref[...] = jnp.zeros_like(acc_ref)

    res = None
    for i in range(n_pages_step):
        start = pl.multiple_of((c * n_pages_step + i) * PAGE, PAGE)
        a48 = a_ref[:, :, pl.ds(start, PAGE)].reshape(N_HEADS_ATT * 8, PAGE).astype(BF16)
        v = jnp.concatenate([v_refs[i][h] for h in range(N_HEADS_ATT)], axis=1).astype(BF16)
        d = _dot(a48, v)
        res = d if res is None else res + d
    for h in range(N_HEADS_ATT):
        acc_ref[h] += _head_tile(res, h)

    @pl.when(c == pl.num_programs(1) - 1)
    def _():
        for h in range(N_HEADS_ATT):
            o = acc_ref[h][0:1] + anew_ref[h][0:1] * vn_ref[h, pl.ds(b, 1), :]
            o_ref[:, h * ATT_HW:(h + 1) * ATT_HW] = _rms(o, sub_ref[...]) * (1.0 - lam_init)


def _attn_pv(scores, q, k_new, v_new, att_lambda, subln, cache_vt, page_table, layer,
             n_pages_step, lam_init):
    bd, n_pages = page_table.shape
    past = n_pages * PAGE
    steps = n_pages // n_pages_step
    const2 = lambda b, c, pt: (0, 0)
    const3 = lambda b, c, pt: (0, 0, 0)
    hm = pl.BlockSpec((N_HEADS_ATT, bd, ATT_HW), const3)
    grid_spec = pltpu.PrefetchScalarGridSpec(
        num_scalar_prefetch=1,
        grid=(bd, steps),
        in_specs=[
            pl.BlockSpec((None, N_HEADS_ATT, 8, past), lambda b, c, pt: (b, 0, 0, 0)),
            hm, hm, hm,
            pl.BlockSpec((4, ATT_DH), const2),
            pl.BlockSpec((1, ATT_HW), const2),
        ] + _page_specs(layer, n_pages_step),
        out_specs=pl.BlockSpec((None, 1, ATT_W), lambda b, c, pt: (b, 0, 0)),
        scratch_shapes=[pltpu.VMEM((N_HEADS_ATT, 8, past), F32),
                        pltpu.VMEM((N_HEADS_ATT, 8, ATT_HW), F32),
                        pltpu.VMEM((N_HEADS_ATT, 8, ATT_HW), F32)],
    )
    out = pl.pallas_call(
        functools.partial(_attn_pv_kernel, n_pages_step=n_pages_step, past=past,
                          lam_init=lam_init, slopes=[float(x) for x in _alibi_slopes(N_HEADS_ATT)]),
        grid_spec=grid_spec,
        out_shape=jax.ShapeDtypeStruct((bd, 1, ATT_W), F32),
        compiler_params=_cparams(("parallel", "arbitrary")),
        name="attn_pv",
    )(page_table, scores, q, k_new, v_new, att_lambda, subln, *([cache_vt] * n_pages_step))
    return out.reshape(bd, ATT_W)


def _mlstm_chunk_kernel(q_ref, k_ref, v_ref, o_ref, g_ref, bias_ref, norm_ref, c0_ref, n0_ref,
                        m0_ref, h_ref, c_ref, n_ref, m_ref, *, L):
    @pl.when(pl.program_id(1) == 0)
    def _():
        c_ref[...] = c0_ref[...]
        n_ref[...] = n0_ref[...]
        m_ref[...] = m0_ref[...]

    g = g_ref[...] + bias_ref[...]
    g_t = g.T
    r = lax.broadcasted_iota(jnp.int32, (L, L), 0)
    c = lax.broadcasted_iota(jnp.int32, (L, L), 1)
    causal = c <= r

    def lanes(x, n):
        reps = [x] * (n // 128) + ([x[:, :n % 128]] if n % 128 else [])
        return reps[0] if len(reps) == 1 else jnp.concatenate(reps, axis=1)

    for h in range(N_HEADS_ML):
        sl = slice(h * ML_DH, (h + 1) * ML_DH)
        ig_r = g_t[h:h + 1, :]
        lf_r = _log_sigmoid(g_t[N_HEADS_ML + h:N_HEADS_ML + h + 1, :])
        ig_c = jnp.broadcast_to(g[:, h:h + 1], (L, 128))
        lf_c = jnp.broadcast_to(_log_sigmoid(g[:, N_HEADS_ML + h:N_HEADS_ML + h + 1]), (L, 128))
        b_c = jnp.broadcast_to(jnp.sum(jnp.where(causal, lf_r, 0.0), axis=1, keepdims=True),
                               (L, 128))
        b_r = jnp.sum(jnp.where(r <= c, lanes(lf_c, L), 0.0), axis=0, keepdims=True)
        m_prev = m_ref[h:h + 1, :]
        d = jnp.where(causal, lanes(b_c, L) + (ig_r - b_r), -jnp.inf)
        inter = b_c + m_prev
        m_t = jnp.maximum(inter, jnp.max(d, axis=1, keepdims=True))
        w_intra = jnp.exp(d - lanes(m_t, L))
        w_state = jnp.exp(inter - m_t)

        q = q_ref[:, sl]
        k = k_ref[:, sl] * (ML_DH ** -0.5)
        v = v_ref[:, sl]
        qb, kb, vb = q.astype(BF16), k.astype(BF16), v.astype(BF16)
        cst = c_ref[h]
        nst = n_ref[h:h + 1, :]
        a = w_intra * _dot_nt(qb, kb)
        num = _dot(a.astype(BF16), vb) + lanes(w_state, ML_DH) * _dot_nt(qb, cst.astype(BF16))
        den = jnp.sum(a, axis=1, keepdims=True) + w_state * jnp.sum(q * nst, axis=1, keepdims=True)
        hh = num * lanes(1.0 / jnp.maximum(jnp.abs(den), jnp.exp(-m_t)), ML_DH)

        m_new = m_t[L - 1:L, :]
        b_last = b_c[L - 1:L, :]
        w_s = lanes(jnp.exp(b_last - b_c + ig_c - m_new), ML_DH)
        decay = lanes(jnp.exp(b_last + m_prev - m_new), ML_DH)
        c_ref[h] = decay * cst + _dot_tn((w_s * v).astype(BF16), kb)
        n_ref[h:h + 1, :] = decay * nst + jnp.sum(w_s * k, axis=0, keepdims=True)
        m_ref[h:h + 1, :] = m_new

        hn = _rms(hh, norm_ref[:, sl])
        h_ref[:, sl] = (_sigmoid(o_ref[:, sl]) * hn).astype(h_ref.dtype)


def _mlstm_prompt(proj, gates, b_if_pad, norm, c0, n0, m0, n_seq, seq, L):
    nc = seq // L
    blk = lambda off: pl.BlockSpec((L, ML_W), lambda b, c, off=off: (b * nc + c, off // ML_W))
    const2 = lambda b, c: (0, 0)
    return pl.pallas_call(
        functools.partial(_mlstm_chunk_kernel, L=L),
        grid=(n_seq, nc),
        in_specs=[
            blk(R_MQ), blk(R_MK), blk(R_MV), blk(R_MO),
            pl.BlockSpec((L, GATE_W), lambda b, c: (b * nc + c, 0)),
            pl.BlockSpec((1, GATE_W), const2),
            pl.BlockSpec((1, ML_W), const2),
            pl.BlockSpec((None, N_HEADS_ML, ML_DH, ML_DH), lambda b, c: (b, 0, 0, 0)),
            pl.BlockSpec((None, N_HEADS_ML, ML_DH), lambda b, c: (b, 0, 0)),
            pl.BlockSpec((None, N_HEADS_ML, 128), lambda b, c: (b, 0, 0)),
        ],
        out_specs=[
            pl.BlockSpec((L, ML_W), lambda b, c: (b * nc + c, 0)),
            pl.BlockSpec((None, N_HEADS_ML, ML_DH, ML_DH), lambda b, c: (b, 0, 0, 0)),
            pl.BlockSpec((None, N_HEADS_ML, ML_DH), lambda b, c: (b, 0, 0)),
            pl.BlockSpec((None, N_HEADS_ML, 128), lambda b, c: (b, 0, 0)),
        ],
        out_shape=[
            jax.ShapeDtypeStruct((n_seq * seq, ML_W), BF16),
            jax.ShapeDtypeStruct((n_seq, N_HEADS_ML, ML_DH, ML_DH), F32),
            jax.ShapeDtypeStruct((n_seq, N_HEADS_ML, ML_DH), F32),
            jax.ShapeDtypeStruct((n_seq, N_HEADS_ML, 128), F32),
        ],
        compiler_params=_cparams(("parallel", "arbitrary")),
        name="mlstm_prompt",
    )(proj, proj, proj, proj, gates, b_if_pad, norm, c0, n0, m0)


def _mlstm_step_kernel(q_ref, k_ref, v_ref, o_ref, g_ref, bias_ref, norm_ref, c0_ref, n0_ref,
                       m0_ref, h_ref, c_ref, n_ref, m_ref):
    b = pl.program_id(0)
    g = g_ref[pl.ds(b, 1), :] + bias_ref[...]
    r = lax.broadcasted_iota(jnp.int32, (ML_DH, ML_DH), 0)
    c = lax.broadcasted_iota(jnp.int32, (ML_DH, ML_DH), 1)
    eye = r == c
    for h in range(N_HEADS_ML):
        sl = slice(h * ML_DH, (h + 1) * ML_DH)
        ig = g[:, h:h + 1]
        lf = _log_sigmoid(g[:, N_HEADS_ML + h:N_HEADS_ML + h + 1])
        m_prev = m0_ref[h:h + 1, 0:1]
        inter = lf + m_prev
        m_t = jnp.maximum(inter, ig)
        w_i = jnp.exp(ig - m_t)
        w_state = jnp.exp(inter - m_t)

        q = q_ref[pl.ds(b, 1), sl]
        k = k_ref[pl.ds(b, 1), sl] * (ML_DH ** -0.5)
        v = v_ref[pl.ds(b, 1), sl]
        cst = c0_ref[h]
        nst = n0_ref[h:h + 1, :]
        a = w_i * jnp.sum(q * k, axis=1, keepdims=True)
        cq = _dot_nt(jnp.broadcast_to(q, (8, ML_DH)).astype(BF16), cst.astype(BF16))[0:1]
        num = a * v + w_state * cq
        den = a + w_state * jnp.sum(q * nst, axis=1, keepdims=True)
        hh = num / jnp.maximum(jnp.abs(den), jnp.exp(-m_t))

        v_col = jnp.sum(jnp.where(eye, w_i * v, 0.0), axis=1, keepdims=True)
        c_ref[h] = w_state * cst + v_col * k
        n_ref[h:h + 1, :] = w_state * nst + w_i * k
        m_ref[h:h + 1, :] = jnp.broadcast_to(m_t, (1, 128))

        hn = _rms(hh, norm_ref[:, sl])
        h_ref[:, sl] = _sigmoid(o_ref[pl.ds(b, 1), sl]) * hn


def _mlstm_step(proj, gates, b_if_pad, norm, c0, n0, m0):
    bd = proj.shape[0]
    blk = lambda off: pl.BlockSpec((bd, ML_W), lambda b, off=off: (0, off // ML_W))
    const2 = lambda b: (0, 0)
    st4 = pl.BlockSpec((None, N_HEADS_ML, ML_DH, ML_DH), lambda b: (b, 0, 0, 0))
    st3 = pl.BlockSpec((None, N_HEADS_ML, ML_DH), lambda b: (b, 0, 0))
    stm = pl.BlockSpec((None, N_HEADS_ML, 128), lambda b: (b, 0, 0))
    out = pl.pallas_call(
        _mlstm_step_kernel,
        grid=(bd,),
        in_specs=[blk(R_MQ), blk(R_MK), blk(R_MV), blk(R_MO),
                  pl.BlockSpec((bd, GATE_W), const2),
                  pl.BlockSpec((1, GATE_W), const2),
                  pl.BlockSpec((1, ML_W), const2),
                  st4, st3, stm],
        out_specs=[pl.BlockSpec((None, 1, ML_W), lambda b: (b, 0, 0)), st4, st3, stm],
        out_shape=[
            jax.ShapeDtypeStruct((bd, 1, ML_W), F32),
            jax.ShapeDtypeStruct((bd, N_HEADS_ML, ML_DH, ML_DH), F32),
            jax.ShapeDtypeStruct((bd, N_HEADS_ML, ML_DH), F32),
            jax.ShapeDtypeStruct((bd, N_HEADS_ML, 128), F32),
        ],
        compiler_params=_cparams(("parallel",)),
        name="mlstm_step",
    )(proj, proj, proj, proj, gates, b_if_pad, norm, c0, n0, m0)
    return (out[0].reshape(bd, ML_W),) + tuple(out[1:])


def _conv3_rows(u, w_ref, prev8):
    w0, w1, w2 = w_ref[0:1, :], w_ref[1:2, :], w_ref[2:3, :]
    y = w0 * pltpu.roll(u, 2, 0) + w1 * pltpu.roll(u, 1, 0) + w2 * u
    head = u[0:8, :]
    row = lax.broadcasted_iota(jnp.int32, head.shape, 0)
    p1 = prev8[7:8, :]
    p2 = prev8[6:7, :]
    h1 = jnp.where(row == 0, p1, pltpu.roll(head, 1, 0))
    h2 = jnp.where(row == 0, p2, jnp.where(row == 1, p1, pltpu.roll(head, 2, 0)))
    return jnp.concatenate([w0 * h2 + w1 * h1 + w2 * head, y[8:, :]], axis=0)


def _conv3_state(u, w_ref, s0, s1):
    return w_ref[0:1, :] * s0 + w_ref[1:2, :] * s1 + w_ref[2:3, :] * u


def _outproj_prompt_kernel(x_ref, a_ref, hm_ref, cb_ref, cc_ref, cx_ref, cw_ref, w_ref,
                           o_ref, st_ref, carry_ref, *, tiles_per_seq):
    i = pl.program_id(0)
    u = cc_ref[...] * cx_ref[...]
    prev8 = jnp.where(i % tiles_per_seq == 0, 0.0, carry_ref[...])
    yc = cb_ref[...] * _conv3_rows(u, cw_ref, prev8)
    tail = u[u.shape[0] - 8:, :]
    carry_ref[...] = tail
    st_ref[...] = tail
    acc = _dot(a_ref[...], w_ref[0:ATT_W, :])
    acc = acc + _dot(hm_ref[...], w_ref[ATT_W:ATT_W + ML_W, :])
    acc = acc + _dot(yc.astype(BF16), w_ref[ATT_W + ML_W:, :])
    o_ref[...] = x_ref[...] + acc


def _outproj_prompt(x, a, hm, proj, conv_w, w_out, wl, n_seq, seq, tm):
    m = x.shape[0]
    tps = seq // tm
    cblk = lambda off: pl.BlockSpec((tm, CONV_C), lambda i, off=off: (i, off // CONV_C))
    return pl.pallas_call(
        functools.partial(_outproj_prompt_kernel, tiles_per_seq=tps),
        grid=(m // tm,),
        in_specs=[
            pl.BlockSpec((tm, D_MODEL), lambda i: (i, 0)),
            pl.BlockSpec((tm, ATT_W), lambda i: (i, 0)),
            pl.BlockSpec((tm, ML_W), lambda i: (i, 0)),
            cblk(R_CB), cblk(R_CC), cblk(R_CX),
            pl.BlockSpec((3, CONV_C), lambda i: (0, 0)),
            pl.BlockSpec((None, D_MODEL, D_MODEL), lambda i: (wl, 0, 0)),
        ],
        out_specs=[
            pl.BlockSpec((tm, D_MODEL), lambda i: (i, 0)),
            pl.BlockSpec((None, 8, CONV_C), lambda i: (i, 0, 0)),
        ],
        out_shape=[
            jax.ShapeDtypeStruct((m, D_MODEL), F32),
            jax.ShapeDtypeStruct((m // tm, 8, CONV_C), F32),
        ],
        scratch_shapes=[pltpu.VMEM((8, CONV_C), F32)],
        compiler_params=_cparams(("arbitrary",)),
        name="outproj_prompt",
    )(x, a, hm, proj, proj, proj, conv_w, w_out)


def _outproj_step_kernel(x_ref, a_ref, hm_ref, cb_ref, cc_ref, cx_ref, cw_ref, s0_ref, s1_ref,
                         w_ref, o_ref, u_ref):
    u = cc_ref[...] * cx_ref[...]
    u_ref[...] = u
    yc = cb_ref[...] * _conv3_state(u, cw_ref, s0_ref[...], s1_ref[...])
    acc = _dot(a_ref[...].astype(BF16), w_ref[0:ATT_W, :])
    acc = acc + _dot(hm_ref[...].astype(BF16), w_ref[ATT_W:ATT_W + ML_W, :])
    acc = acc + _dot(yc.astype(BF16), w_ref[ATT_W + ML_W:, :])
    o_ref[...] = x_ref[...] + acc


def _outproj_step(x, a, hm, proj, conv_w, s0, s1, w_out, wl):
    bd = x.shape[0]
    full = lambda shp: pl.BlockSpec(shp, lambda i: (0, 0))
    cblk = lambda off: pl.BlockSpec((bd, CONV_C), lambda i, off=off: (0, off // CONV_C))
    return pl.pallas_call(
        _outproj_step_kernel,
        grid=(1,),
        in_specs=[full((bd, D_MODEL)), full((bd, ATT_W)), full((bd, ML_W)),
                  cblk(R_CB), cblk(R_CC), cblk(R_CX),
                  full((3, CONV_C)), full((bd, CONV_C)), full((bd, CONV_C)),
                  pl.BlockSpec((None, D_MODEL, D_MODEL), lambda i: (wl, 0, 0))],
        out_specs=[full((bd, D_MODEL)), full((bd, CONV_C))],
        out_shape=[jax.ShapeDtypeStruct((bd, D_MODEL), F32),
                   jax.ShapeDtypeStruct((bd, CONV_C), F32)],
        compiler_params=_cparams(("arbitrary",)),
        name="outproj_step",
    )(x, a, hm, proj, proj, proj, conv_w, s0, s1, w_out)


def _ffn_prompt_kernel(x_ref, g_ref, wg_ref, wv_ref, cwg_ref, cwv_ref, wd_ref, gfin_ref,
                       o_ref, stg_ref, stv_ref, h_ref, carry_ref, *, tiles_per_seq, final_norm):
    i = pl.program_id(0)
    j = pl.program_id(1)

    @pl.when(j == 0)
    def _():
        x = x_ref[...]
        h_ref[...] = _rms(x, g_ref[...]).astype(BF16)
        o_ref[...] = x

    h = h_ref[...]
    ug = _dot(h, wg_ref[...])
    uv = _dot(h, wv_ref[...])
    start = i % tiles_per_seq == 0
    pg = jnp.where(start, 0.0, carry_ref[j, 0])
    pv = jnp.where(start, 0.0, carry_ref[j, 1])
    cg = _conv3_rows(ug, cwg_ref, pg)
    cv = _conv3_rows(uv, cwv_ref, pv)
    tm = ug.shape[0]
    tg = ug[tm - 8:, :]
    tv = uv[tm - 8:, :]
    carry_ref[j, 0] = tg
    carry_ref[j, 1] = tv
    stg_ref[...] = tg
    stv_ref[...] = tv
    act = (cg * _sigmoid(cg) * cv).astype(BF16)
    o_ref[...] += _dot(act, wd_ref[...])
    if final_norm:
        @pl.when(j == pl.num_programs(1) - 1)
        def _():
            o_ref[...] = _rms(o_ref[...], gfin_ref[...])


def _ffn_prompt(x, g, w_up, ffn_conv_w, w_down, wl, g_final, final_norm, n_seq, seq, tm, tf):
    m = x.shape[0]
    tps = seq // tm
    nj = D_FF // tf
    return pl.pallas_call(
        functools.partial(_ffn_prompt_kernel, tiles_per_seq=tps, final_norm=final_norm),
        grid=(m // tm, nj),
        in_specs=[
            pl.BlockSpec((tm, D_MODEL), lambda i, j: (i, 0)),
            pl.BlockSpec((1, D_MODEL), lambda i, j: (0, 0)),
            pl.BlockSpec((None, D_MODEL, tf), lambda i, j: (wl, 0, j)),
            pl.BlockSpec((None, D_MODEL, tf), lambda i, j: (wl, 0, nj + j)),
            pl.BlockSpec((3, tf), lambda i, j: (0, j)),
            pl.BlockSpec((3, tf), lambda i, j: (0, nj + j)),
            pl.BlockSpec((None, tf, D_MODEL), lambda i, j: (wl, j, 0)),
            pl.BlockSpec((1, D_MODEL), lambda i, j: (0, 0)),
        ],
        out_specs=[
            pl.BlockSpec((tm, D_MODEL), lambda i, j: (i, 0)),
            pl.BlockSpec((None, 8, tf), lambda i, j: (i, 0, j)),
            pl.BlockSpec((None, 8, tf), lambda i, j: (i, 0, j)),
        ],
        out_shape=[
            jax.ShapeDtypeStruct((m, D_MODEL), F32),
            jax.ShapeDtypeStruct((m // tm, 8, D_FF), F32),
            jax.ShapeDtypeStruct((m // tm, 8, D_FF), F32),
        ],
        scratch_shapes=[pltpu.VMEM((tm, D_MODEL), BF16), pltpu.VMEM((nj, 2, 8, tf), F32)],
        compiler_params=_cparams(("arbitrary", "arbitrary")),
        name="ffn_prompt",
    )(x, g, w_up, w_up, ffn_conv_w, ffn_conv_w, w_down, g_final)


def _ffn_step_kernel(x_ref, g_ref, wg_ref, wv_ref, cwg_ref, cwv_ref, s0g_ref, s1g_ref, s0v_ref,
                     s1v_ref, wd_ref, gfin_ref, o_ref, ug_ref, uv_ref, h_ref, *, final_norm):
    @pl.when(pl.program_id(0) == 0)
    def _():
        x = x_ref[...]
        h_ref[...] = _rms(x, g_ref[...]).astype(BF16)
        o_ref[...] = x

    h = h_ref[...]
    ug = _dot(h, wg_ref[...])
    uv = _dot(h, wv_ref[...])
    ug_ref[...] = ug
    uv_ref[...] = uv
    cg = _conv3_state(ug, cwg_ref, s0g_ref[...], s1g_ref[...])
    cv = _conv3_state(uv, cwv_ref, s0v_ref[...], s1v_ref[...])
    act = (cg * _sigmoid(cg) * cv).astype(BF16)
    o_ref[...] += _dot(act, wd_ref[...])
    if final_norm:
        @pl.when(pl.program_id(0) == pl.num_programs(0) - 1)
        def _():
            o_ref[...] = _rms(o_ref[...], gfin_ref[...])


def _ffn_step(x, g, w_up, ffn_conv_w, s0, s1, w_down, wl, g_final, final_norm, tf):
    bd = x.shape[0]
    nj = D_FF // tf
    gcol = lambda rows: pl.BlockSpec((rows, tf), lambda j: (0, j))
    vcol = lambda rows: pl.BlockSpec((rows, tf), lambda j: (0, nj + j))
    return pl.pallas_call(
        functools.partial(_ffn_step_kernel, final_norm=final_norm),
        grid=(nj,),
        in_specs=[
            pl.BlockSpec((bd, D_MODEL), lambda j: (0, 0)),
            pl.BlockSpec((1, D_MODEL), lambda j: (0, 0)),
            pl.BlockSpec((None, D_MODEL, tf), lambda j: (wl, 0, j)),
            pl.BlockSpec((None, D_MODEL, tf), lambda j: (wl, 0, nj + j)),
            gcol(3), vcol(3),
            gcol(bd), gcol(bd), vcol(bd), vcol(bd),
            pl.BlockSpec((None, tf, D_MODEL), lambda j: (wl, j, 0)),
            pl.BlockSpec((1, D_MODEL), lambda j: (0, 0)),
        ],
        out_specs=[pl.BlockSpec((bd, D_MODEL), lambda j: (0, 0)), gcol(bd), gcol(bd)],
        out_shape=[jax.ShapeDtypeStruct((bd, D_MODEL), F32),
                   jax.ShapeDtypeStruct((bd, D_FF), F32),
                   jax.ShapeDtypeStruct((bd, D_FF), F32)],
        scratch_shapes=[pltpu.VMEM((bd, D_MODEL), BF16)],
        compiler_params=_cparams(("arbitrary",)),
        name="ffn_step",
    )(x, g, w_up, w_up, ffn_conv_w, ffn_conv_w, s0, s1, s0, s1, w_down, g_final)


def _tiles(seq):
    tm = min(512, seq)
    tmi = min(1024, seq)
    tf = 512
    tq = min(256, seq)
    rs = tq
    lc = min(512, seq)
    return tm, tmi, tf, tq, rs, lc


def kernel(x_prompt, x_sample, cache_k, cache_v, state_mlstm_c, state_mlstm_n, state_mlstm_m,
           state_conv, state_ffn_conv, page_table,
           g_mix, w_in, b_if, att_lambda, att_subln, mlstm_norm, conv_w, w_out,
           g_ffn, w_up, ffn_conv_w, w_down, g_final):
    depth = w_in.shape[0]
    n_seq, seq, _ = x_prompt.shape
    bd = x_sample.shape[0]
    n_pool = cache_k.shape[1]
    tm, tmi, tf, tq, rs, lc = _tiles(seq)
    assert seq % tm == 0 and seq % tmi == 0 and seq % lc == 0 and tm % 8 == 0
    assert seq % (2 * tq) == 0 and tq % ATT_HW == 0, "attention pairs query tiles of whole lane tiles"
    assert D_FF % tf == 0 and page_table.shape[1] >= 1
    tps = seq // tm
    n_pages = page_table.shape[1]
    pages_step = math.gcd(32, n_pages)

    n_gate = 2 * N_HEADS_ML
    w_bf = w_in.transpose(0, 2, 1).astype(BF16)
    w_conv = w_bf[:, N_QKV_MLSTM + n_gate:, :]
    w_out_b = w_out.astype(BF16)
    w_up_b = w_up.astype(BF16)
    w_down_b = w_down.astype(BF16)
    b_if_pad = jnp.pad(b_if, ((0, 0), (0, GATE_W - n_gate))).reshape(depth, 1, GATE_W)

    slopes = jnp.asarray(_alibi_slopes(N_HEADS_ATT))
    cache_kt = cache_k.transpose(0, 1, 3, 2, 4)
    cache_vt = cache_v.transpose(0, 1, 3, 2, 4)

    c0p = jnp.zeros((n_seq, N_HEADS_ML, ML_DH, ML_DH), F32)
    n0p = jnp.zeros((n_seq, N_HEADS_ML, ML_DH), F32)
    m0p = jnp.full((n_seq, N_HEADS_ML, 128), -jnp.inf, F32)

    xp = x_prompt.reshape(n_seq * seq, D_MODEL)
    xs = x_sample.reshape(bd, D_MODEL)
    outs_p = [[] for _ in range(7)]
    outs_s = [[] for _ in range(7)]
    k_all = v_all = None
    gfin = g_final.reshape(1, D_MODEL)
    for l in range(depth):
        last = l == depth - 1
        lam_init = 0.8 - 0.6 * math.exp(-0.3 * l)
        gm = g_mix[l].reshape(1, D_MODEL)
        gf = g_ffn[l].reshape(1, D_MODEL)
        sub = att_subln[l].reshape(1, ATT_HW)
        mnorm = mlstm_norm[l].reshape(1, ML_W)

        proj, gates, q, k_all, v_all = _inproj(xp, gm, w_bf, w_conv, l, n_seq, seq, tmi, l, depth,
                                               k_all, v_all)
        a = _attn_prompt(q, k_all, v_all, l, slopes, att_lambda[l], sub, n_seq, seq, tq, rs, lam_init)
        hm, c1, n1, m1 = _mlstm_prompt(proj, gates, b_if_pad[l], mnorm, c0p, n0p, m0p, n_seq, seq, lc)
        xp, cst = _outproj_prompt(xp, a, hm, proj, conv_w[l], w_out_b, l, n_seq, seq, tm)
        xp, stg, stv = _ffn_prompt(xp, gf, w_up_b, ffn_conv_w[l], w_down_b, l, gfin, last,
                                   n_seq, seq, tm, tf)
        outs_p[2].append(c1)
        outs_p[3].append(n1)
        outs_p[4].append(m1[:, :, 0])
        outs_p[5].append(cst[tps - 1::tps, 6:8, :])
        outs_p[6].append(jnp.concatenate([stg[tps - 1::tps, 6:8, :], stv[tps - 1::tps, 6:8, :]], axis=-1))

        proj_s, gates_s, q_s, k_s, v_s = _inproj(xs, gm, w_bf, w_conv, l, 1, bd, bd, 0, 1)
        q_s, k_s, v_s = q_s[0], k_s[0, 0], v_s[0, 0]
        scores = _attn_scores(q_s, cache_kt, page_table, l, pages_step)
        a_s = _attn_pv(scores, q_s, k_s, v_s, att_lambda[l], sub, cache_vt, page_table, l,
                       pages_step, lam_init)
        m0s = jnp.broadcast_to(state_mlstm_m[l][:, :, None], (bd, N_HEADS_ML, 128))
        hm_s, c1s, n1s, m1s = _mlstm_step(proj_s, gates_s, b_if_pad[l], mnorm, state_mlstm_c[l],
                                          state_mlstm_n[l], m0s)
        sc = state_conv[l]
        xs, u_s = _outproj_step(xs, a_s, hm_s, proj_s, conv_w[l], sc[:, 0], sc[:, 1], w_out_b, l)
        sf = state_ffn_conv[l]
        xs, ug_s, uv_s = _ffn_step(xs, gf, w_up_b, ffn_conv_w[l], sf[:, 0], sf[:, 1], w_down_b, l,
                                   gfin, last, tf)
        outs_s[0].append(k_s.transpose(1, 0, 2).reshape(bd, 1, N_HEADS_ATT, ATT_HW))
        outs_s[1].append(v_s.transpose(1, 0, 2).reshape(bd, 1, N_HEADS_ATT, ATT_HW))
        outs_s[2].append(c1s)
        outs_s[3].append(n1s)
        outs_s[4].append(m1s[:, :, 0])
        outs_s[5].append(jnp.stack([sc[:, 1], u_s], axis=1))
        outs_s[6].append(jnp.stack([sf[:, 1], jnp.concatenate([ug_s, uv_s], axis=-1)], axis=1))

    y_prompt = xp.reshape(n_seq, seq, D_MODEL)
    y_sample = xs.reshape(bd, 1, D_MODEL)
    c_p, n_p, m_p, conv_p, ffn_p = [jnp.stack(o, 0) for o in outs_p[2:]]
    k_p = k_all.transpose(0, 1, 3, 2, 4)
    v_p = v_all.transpose(0, 1, 3, 2, 4)
    k_s_, v_s_, c_s, n_s, m_s, conv_s, ffn_s = [jnp.stack(o, 0) for o in outs_s]
    return (y_prompt, y_sample, k_p, v_p, k_s_, v_s_, c_p, n_p, m_p, c_s, n_s, m_s,
            conv_p, conv_s, ffn_p, ffn_s)
```

```python
import functools
import math

import numpy as np
import jax
import jax.numpy as jnp
from jax import lax
from jax.experimental import pallas as pl
from jax.experimental.pallas import tpu as pltpu

F32 = jnp.float32
BF16 = jnp.bfloat16

D_MODEL = 2048
N_HEADS_ATT = 6
ATT_DH = 64
ATT_HW = 2 * ATT_DH
ATT_W = N_HEADS_ATT * ATT_HW
N_HEADS_ML = 4
ML_DH = 192
ML_W = N_HEADS_ML * ML_DH
CONV_C = D_MODEL - ATT_W - ML_W
D_FF = 5632
PAGE = 128
EPS = 1e-6
LOG2E = 1.4426950408889634

GATE_W = 128

VMEM_LIMIT = 56 * 1024 * 1024


def _cparams(sem):
    return pltpu.CompilerParams(dimension_semantics=sem, vmem_limit_bytes=VMEM_LIMIT)


def _alibi_slopes(n_heads):
    def geometric(n):
        start = 2.0 ** (-8.0 / n)
        return [start ** (i + 1) for i in range(n)]
    closest = 2 ** int(math.floor(math.log2(n_heads)))
    slopes = geometric(closest)
    if closest < n_heads:
        slopes = slopes + geometric(2 * closest)[0::2][: n_heads - closest]
    return np.asarray(slopes, dtype=np.float32)


def _rms(x, g):
    return x * lax.rsqrt(jnp.mean(x * x, axis=-1, keepdims=True) + EPS) * g


def _log_sigmoid(x):
    return jnp.minimum(x, 0.0) - jnp.log1p(jnp.exp(-jnp.abs(x)))


def _sigmoid(x):
    return 0.5 * (jnp.tanh(0.5 * x) + 1.0)


def _lambda(al_ref, lam_init):
    al = al_ref[...]
    s01 = jnp.sum(al[0:1] * al[1:2], axis=-1, keepdims=True)
    s23 = jnp.sum(al[2:3] * al[3:4], axis=-1, keepdims=True)
    return jnp.exp(s01) - jnp.exp(s23) + lam_init


def _dot_nt(a, b):
    return lax.dot_general(a, b, (((1,), (1,)), ((), ())), preferred_element_type=F32)


def _dot_tn(a, b):
    return lax.dot_general(a, b, (((0,), (0,)), ((), ())), preferred_element_type=F32)


def _dot(a, b):
    return jnp.dot(a, b, preferred_element_type=F32)


TN = ATT_W
N_QKV_MLSTM = 3 * ATT_W + 4 * ML_W
J_ML = 3
J_CONV = N_QKV_MLSTM // TN
N_REST = 4 * ML_W + 3 * CONV_C
R_MQ, R_MK, R_MV, R_MO = 0, ML_W, 2 * ML_W, 3 * ML_W
R_CB = 4 * ML_W
R_CC, R_CX = R_CB + CONV_C, R_CB + 2 * CONV_C


def _inproj_kernel(*refs, layer_slot):
    if layer_slot:
        refs = refs[2:]
    x_ref, g_ref, w_ref, wc_ref, wg_ref, rest_ref, og_ref, q_ref, k_ref, v_ref, h_ref = refs
    j = pl.program_id(2)

    @pl.when(j == 0)
    def _():
        h = _rms(x_ref[...], g_ref[...]).astype(BF16)
        h_ref[...] = h
        og_ref[...] = _dot_nt(h, wg_ref[...])

    for t, ref in enumerate((q_ref, k_ref, v_ref)):
        @pl.when(j == t)
        def _(ref=ref):
            res = _dot_nt(h_ref[...], w_ref[...])
            for h in range(N_HEADS_ATT):
                ref[h] = res[:, h * ATT_HW:(h + 1) * ATT_HW]

    @pl.when((j >= J_ML) & (j < J_CONV))
    def _():
        rest_ref[...] = _dot_nt(h_ref[...], w_ref[...])

    @pl.when(j >= J_CONV)
    def _():
        rest_ref[...] = _dot_nt(h_ref[...], wc_ref[...])


def _inproj(x, g, w_bf, w_conv, wl, n_seq, seq, tm, layer, depth, k_all=None, v_all=None):
    m = x.shape[0]
    tps = seq // tm
    nj = J_CONV + 3 * CONV_C // TN
    alias = k_all is not None
    rest_map = lambda b, i, j: (b * tps + i, jnp.maximum(j - J_ML, 0))
    row = lambda b, i, j: (b * tps + i, 0)
    const = lambda b, i, j: (0, 0)
    once = pl.Buffered(1)
    kv_spec = pl.BlockSpec((None, None, N_HEADS_ATT, tm, ATT_HW), lambda b, i, j: (layer, b, 0, i, 0),
                           pipeline_mode=once)
    in_specs = [
        pl.BlockSpec((tm, D_MODEL), row),
        pl.BlockSpec((1, D_MODEL), const),
        pl.BlockSpec((None, TN, D_MODEL), lambda b, i, j: (wl, jnp.minimum(j, J_CONV - 1), 0)),
        pl.BlockSpec((None, TN, D_MODEL), lambda b, i, j: (wl, jnp.maximum(j - J_CONV, 0), 0)),
        pl.BlockSpec((None, GATE_W, D_MODEL), lambda b, i, j: (wl, N_QKV_MLSTM // GATE_W, 0)),
    ]
    args = [x, g, w_bf, w_conv, w_bf]
    if alias:
        in_specs = [pl.BlockSpec(memory_space=pl.ANY), pl.BlockSpec(memory_space=pl.ANY)] + in_specs
        args = [k_all, v_all] + args
    kv_shape = jax.ShapeDtypeStruct((depth, n_seq, N_HEADS_ATT, seq, ATT_HW), F32)
    return pl.pallas_call(
        functools.partial(_inproj_kernel, layer_slot=alias),
        grid=(n_seq, tps, nj),
        in_specs=in_specs,
        out_specs=[
            pl.BlockSpec((tm, TN), rest_map),
            pl.BlockSpec((tm, GATE_W), row),
            pl.BlockSpec((None, N_HEADS_ATT, tm, ATT_HW), lambda b, i, j: (b, 0, i, 0), pipeline_mode=once),
            kv_spec, kv_spec,
        ],
        out_shape=[
            jax.ShapeDtypeStruct((m, N_REST), F32),
            jax.ShapeDtypeStruct((m, GATE_W), F32),
            jax.ShapeDtypeStruct((n_seq, N_HEADS_ATT, seq, ATT_HW), F32),
            kv_shape, kv_shape,
        ],
        scratch_shapes=[pltpu.VMEM((tm, D_MODEL), BF16)],
        input_output_aliases={0: 3, 1: 4} if alias else {},
        compiler_params=_cparams(("parallel", "parallel", "arbitrary")),
        name="inproj",
    )(*args)


def _attn_prompt_kernel(slopes_ref, qa_ref, qb_ref, k_ref, v_ref, al_ref, sub_ref, oa_ref, ob_ref,
                        q2_ref, kb_ref, vb_ref, m_ref, l_ref, acc_ref, *, tq, rs, nq, lam_init):
    h = pl.program_id(1)
    i = pl.program_id(2)
    slope = slopes_ref[h]
    tile_a, tile_b = i, nq - 1 - i

    @pl.when(i == 0)
    def _():
        kb_ref[...] = k_ref[...].astype(BF16)
        vb_ref[...] = v_ref[...].astype(BF16)

    lane = lax.broadcasted_iota(jnp.int32, (tq, ATT_HW), 1)
    for n, ref in enumerate((qa_ref, qb_ref)):
        q = ref[...] * (ATT_DH ** -0.5 * LOG2E)
        q2_ref[2 * n * tq:(2 * n + 1) * tq, :] = jnp.where(lane < ATT_DH, q, 0.0).astype(BF16)
        q2_ref[(2 * n + 1) * tq:(2 * n + 2) * tq, :] = jnp.where(lane >= ATT_DH, q, 0.0).astype(BF16)
    m_ref[...] = jnp.full_like(m_ref, -jnp.inf)
    l_ref[...] = jnp.zeros_like(l_ref)
    acc_ref[...] = jnp.zeros_like(acc_ref)
    col = lax.broadcasted_iota(jnp.int32, (1, tq), 1)

    def block(row_base, tile, kblk, diag):
        start = pl.multiple_of(kblk * tq, tq)
        bias = (slope * LOG2E) * (col + (kblk - tile) * tq).astype(F32)
        for t in range(2 * tq // rs):
            r0 = row_base + t * rs
            rows = pl.ds(r0 if isinstance(r0, int) else pl.multiple_of(r0, rs), rs)
            q0 = (t * rs) % tq
            nk = min(tq, -(-(q0 + rs) // ATT_HW) * ATT_HW) if diag else tq
            k = kb_ref[pl.ds(start, nk), :]
            v = vb_ref[pl.ds(start, nk), :]
            s = _dot_nt(q2_ref[rows, :], k) + bias[:, :nk]
            if diag:
                r = (lax.broadcasted_iota(jnp.int32, (rs, nk), 0) + t * rs) % tq
                c = lax.broadcasted_iota(jnp.int32, (rs, nk), 1)
                s = jnp.where(c <= r, s, -jnp.inf)
            m_old = m_ref[rows, :]
            m_new = jnp.maximum(m_old, jnp.max(s, axis=-1, keepdims=True))
            alpha = jnp.exp2(m_old - m_new)
            p = jnp.exp2(s - jnp.tile(m_new, (1, nk // ATT_HW)))
            l_ref[rows, :] = alpha * l_ref[rows, :] + jnp.sum(p, axis=-1, keepdims=True)
            acc_ref[rows, :] = alpha * acc_ref[rows, :] + _dot(p.astype(BF16), v)
            m_ref[rows, :] = m_new

    for u in range(nq - 1):
        to_b = (u >= i).astype(jnp.int32)
        block(to_b * (2 * tq), jnp.where(u >= i, tile_b, tile_a), u - to_b * i, False)
    block(0, tile_a, tile_a, True)
    block(2 * tq, tile_b, tile_b, True)

    lam = _lambda(al_ref, lam_init)
    o = acc_ref[...] / l_ref[...]
    for n, ref in enumerate((oa_ref, ob_ref)):
        on = o[2 * n * tq:(2 * n + 1) * tq] - lam * o[(2 * n + 1) * tq:(2 * n + 2) * tq]
        ref[...] = (_rms(on, sub_ref[...]) * (1.0 - lam_init)).astype(ref.dtype)


def _attn_prompt(q, k_all, v_all, layer, slopes, att_lambda, subln, n_seq, seq, tq, rs, lam_init):
    nq = seq // tq
    half = nq // 2
    w = ATT_HW
    kv_spec = pl.BlockSpec((None, None, None, seq, w), lambda b, h, i: (layer, b, h, 0, 0))
    o_shape = jax.ShapeDtypeStruct((n_seq, half, tq, ATT_W), BF16)
    lo, hi = pl.pallas_call(
        functools.partial(_attn_prompt_kernel, tq=tq, rs=rs, nq=nq, lam_init=lam_init),
        grid=(n_seq, N_HEADS_ATT, half),
        in_specs=[
            pl.BlockSpec(memory_space=pltpu.SMEM),
            pl.BlockSpec((None, None, tq, w), lambda b, h, i: (b, h, i, 0)),
            pl.BlockSpec((None, None, tq, w), lambda b, h, i: (b, h, nq - 1 - i, 0)),
            kv_spec, kv_spec,
            pl.BlockSpec((4, ATT_DH), lambda b, h, i: (0, 0)),
            pl.BlockSpec((1, w), lambda b, h, i: (0, 0)),
        ],
        out_specs=[pl.BlockSpec((None, None, tq, w), lambda b, h, i: (b, i, 0, h)),
                   pl.BlockSpec((None, None, tq, w), lambda b, h, i: (b, half - 1 - i, 0, h))],
        out_shape=[o_shape, o_shape],
        scratch_shapes=[pltpu.VMEM((4 * tq, w), BF16), pltpu.VMEM((seq, w), BF16),
                        pltpu.VMEM((seq, w), BF16), pltpu.VMEM((4 * tq, w), F32),
                        pltpu.VMEM((4 * tq, w), F32), pltpu.VMEM((4 * tq, w), F32)],
        compiler_params=_cparams(("parallel", "parallel", "arbitrary")),
        name="attn_prompt",
    )(slopes, q, q, k_all, v_all, att_lambda, subln)
    return jnp.concatenate([lo, hi], axis=1).reshape(n_seq * seq, ATT_W)


def _q_two_maps(q_row):
    r = lax.broadcasted_iota(jnp.int32, (8, ATT_HW), 0)
    c = lax.broadcasted_iota(jnp.int32, (8, ATT_HW), 1)
    keep = ((r == 0) & (c < ATT_DH)) | ((r == 1) & (c >= ATT_DH))
    return jnp.where(keep, q_row * (ATT_DH ** -0.5), 0.0)


def _q_rows(q_ref, b):
    return jnp.concatenate([_q_two_maps(q_ref[h, pl.ds(b, 1), :]) for h in range(N_HEADS_ATT)], axis=0)


def _head_tile(res, h):
    return res[8 * h:8 * h + 8, h * ATT_HW:(h + 1) * ATT_HW]


def _scores_kernel(pt_ref, q_ref, *refs, n_pages_step):
    k_refs = refs[:n_pages_step]
    s_ref = refs[n_pages_step]
    b = pl.program_id(0)
    q48 = _q_rows(q_ref, b).astype(BF16)
    for i in range(n_pages_step):
        k = k_refs[i][...].reshape(N_HEADS_ATT * PAGE, ATT_HW).astype(BF16)
        res = _dot_nt(q48, k)
        for h in range(N_HEADS_ATT):
            s_ref[h, :, i * PAGE:(i + 1) * PAGE] = _head_tile(res, h)


def _page_specs(layer, n_pages_step):
    def pmap(i):
        return lambda b, c, pt: (layer, pt[b, c * n_pages_step + i], 0, 0, 0)
    return [pl.BlockSpec((None, None, N_HEADS_ATT, PAGE, ATT_HW), pmap(i)) for i in range(n_pages_step)]


def _attn_scores(q, cache_kt, page_table, layer, n_pages_step):
    bd, n_pages = page_table.shape
    steps = n_pages // n_pages_step
    grid_spec = pltpu.PrefetchScalarGridSpec(
        num_scalar_prefetch=1,
        grid=(bd, steps),
        in_specs=[pl.BlockSpec((N_HEADS_ATT, bd, ATT_HW), lambda b, c, pt: (0, 0, 0))]
        + _page_specs(layer, n_pages_step),
        out_specs=pl.BlockSpec((None, N_HEADS_ATT, 8, n_pages_step * PAGE),
                               lambda b, c, pt: (b, 0, 0, c)),
    )
    return pl.pallas_call(
        functools.partial(_scores_kernel, n_pages_step=n_pages_step),
        grid_spec=grid_spec,
        out_shape=jax.ShapeDtypeStruct((bd, N_HEADS_ATT, 8, n_pages * PAGE), F32),
        compiler_params=_cparams(("parallel", "arbitrary")),
        name="attn_scores",
    )(page_table, q, *([cache_kt] * n_pages_step))


def _attn_pv_kernel(pt_ref, s_ref, q_ref, kn_ref, vn_ref, al_ref, sub_ref, *refs,
                    n_pages_step, past, lam_init, slopes):
    v_refs = refs[:n_pages_step]
    o_ref = refs[n_pages_step]
    a_ref, anew_ref, acc_ref = refs[n_pages_step + 1:]
    b = pl.program_id(0)
    c = pl.program_id(1)

    @pl.when(c == 0)
    def _():
        lam = _lambda(al_ref, lam_init)
        dist = (past - lax.broadcasted_iota(jnp.int32, (1, past), 1)).astype(F32)
        for h in range(N_HEADS_ATT):
            s = s_ref[h] - slopes[h] * dist
            qk = _q_two_maps(q_ref[h, pl.ds(b, 1), :]) * kn_ref[h, pl.ds(b, 1), :]
            s_new = jnp.sum(qk, axis=-1, keepdims=True)
            m = jnp.maximum(jnp.max(s, axis=-1, keepdims=True), s_new)
            p = jnp.exp(s - m)
            p_new = jnp.exp(s_new - m)
            l = jnp.sum(p, axis=-1, keepdims=True) + p_new
            p = p / l
            p_new = p_new / l
            a_ref[h] = jnp.broadcast_to(p[0:1] - lam * p[1:2], (8, past))
            anew_ref[h] = jnp.broadcast_to(p_new[0:1] - lam * p_new[1:2], (8, ATT_HW))
        acc_ref[...] = jnp.zeros_like(acc_ref)

    res = None
    for i in range(n_pages_step):
        start = pl.multiple_of((c * n_pages_step + i) * PAGE, PAGE)
        a48 = a_ref[:, :, pl.ds(start, PAGE)].reshape(N_HEADS_ATT * 8, PAGE).astype(BF16)
        v = jnp.concatenate([v_refs[i][h] for h in range(N_HEADS_ATT)], axis=1).astype(BF16)
        d = _dot(a48, v)
        res = d if res is None else res + d
    for h in range(N_HEADS_ATT):
        acc_ref[h] += _head_tile(res, h)

    @pl.when(c == pl.num_programs(1) - 1)
    def _():
        for h in range(N_HEADS_ATT):
            o = acc_ref[h][0:1] + anew_ref[h][0:1] * vn_ref[h, pl.ds(b, 1), :]
            o_ref[:, h * ATT_HW:(h + 1) * ATT_HW] = _rms(o, sub_ref[...]) * (1.0 - lam_init)


def _attn_pv(scores, q, k_new, v_new, att_lambda, subln, cache_vt, page_table, layer,
             n_pages_step, lam_init):
    bd, n_pages = page_table.shape
    past = n_pages * PAGE
    steps = n_pages // n_pages_step
    const2 = lambda b, c, pt: (0, 0)
    const3 = lambda b, c, pt: (0, 0, 0)
    hm = pl.BlockSpec((N_HEADS_ATT, bd, ATT_HW), const3)
    grid_spec = pltpu.PrefetchScalarGridSpec(
        num_scalar_prefetch=1,
        grid=(bd, steps),
        in_specs=[
            pl.BlockSpec((None, N_HEADS_ATT, 8, past), lambda b, c, pt: (b, 0, 0, 0)),
            hm, hm, hm,
            pl.BlockSpec((4, ATT_DH), const2),
            pl.BlockSpec((1, ATT_HW), const2),
        ] + _page_specs(layer, n_pages_step),
        out_specs=pl.BlockSpec((None, 1, ATT_W), lambda b, c, pt: (b, 0, 0)),
        scratch_shapes=[pltpu.VMEM((N_HEADS_ATT, 8, past), F32),
                        pltpu.VMEM((N_HEADS_ATT, 8, ATT_HW), F32),
                        pltpu.VMEM((N_HEADS_ATT, 8, ATT_HW), F32)],
    )
    out = pl.pallas_call(
        functools.partial(_attn_pv_kernel, n_pages_step=n_pages_step, past=past,
                          lam_init=lam_init, slopes=[float(x) for x in _alibi_slopes(N_HEADS_ATT)]),
        grid_spec=grid_spec,
        out_shape=jax.ShapeDtypeStruct((bd, 1, ATT_W), F32),
        compiler_params=_cparams(("parallel", "arbitrary")),
        name="attn_pv",
    )(page_table, scores, q, k_new, v_new, att_lambda, subln, *([cache_vt] * n_pages_step))
    return out.reshape(bd, ATT_W)


def _mlstm_chunk_kernel(q_ref, k_ref, v_ref, o_ref, g_ref, bias_ref, norm_ref, c0_ref, n0_ref,
                        m0_ref, h_ref, c_ref, n_ref, m_ref, *, L):
    @pl.when(pl.program_id(1) == 0)
    def _():
        c_ref[...] = c0_ref[...]
        n_ref[...] = n0_ref[...]
        m_ref[...] = m0_ref[...]

    g = g_ref[...] + bias_ref[...]
    g_t = g.T
    r = lax.broadcasted_iota(jnp.int32, (L, L), 0)
    c = lax.broadcasted_iota(jnp.int32, (L, L), 1)
    causal = c <= r

    def lanes(x, n):
        reps = [x] * (n // 128) + ([x[:, :n % 128]] if n % 128 else [])
        return reps[0] if len(reps) == 1 else jnp.concatenate(reps, axis=1)

    for h in range(N_HEADS_ML):
        sl = slice(h * ML_DH, (h + 1) * ML_DH)
        ig_r = g_t[h:h + 1, :]
        lf_r = _log_sigmoid(g_t[N_HEADS_ML + h:N_HEADS_ML + h + 1, :])
        ig_c = jnp.broadcast_to(g[:, h:h + 1], (L, 128))
        lf_c = jnp.broadcast_to(_log_sigmoid(g[:, N_HEADS_ML + h:N_HEADS_ML + h + 1]), (L, 128))
        b_c = jnp.broadcast_to(jnp.sum(jnp.where(causal, lf_r, 0.0), axis=1, keepdims=True),
                               (L, 128))
        b_r = jnp.sum(jnp.where(r <= c, lanes(lf_c, L), 0.0), axis=0, keepdims=True)
        m_prev = m_ref[h:h + 1, :]
        d = jnp.where(causal, lanes(b_c, L) + (ig_r - b_r), -jnp.inf)
        inter = b_c + m_prev
        m_t = jnp.maximum(inter, jnp.max(d, axis=1, keepdims=True))
        w_intra = jnp.exp(d - lanes(m_t, L))
        w_state = jnp.exp(inter - m_t)

        q = q_ref[:, sl]
        k = k_ref[:, sl] * (ML_DH ** -0.5)
        v = v_ref[:, sl]
        qb, kb, vb = q.astype(BF16), k.astype(BF16), v.astype(BF16)
        cst = c_ref[h]
        nst = n_ref[h:h + 1, :]
        a = w_intra * _dot_nt(qb, kb)
        num = _dot(a.astype(BF16), vb) + lanes(w_state, ML_DH) * _dot_nt(qb, cst.astype(BF16))
        den = jnp.sum(a, axis=1, keepdims=True) + w_state * jnp.sum(q * nst, axis=1, keepdims=True)
        hh = num * lanes(1.0 / jnp.maximum(jnp.abs(den), jnp.exp(-m_t)), ML_DH)

        m_new = m_t[L - 1:L, :]
        b_last = b_c[L - 1:L, :]
        w_s = lanes(jnp.exp(b_last - b_c + ig_c - m_new), ML_DH)
        decay = lanes(jnp.exp(b_last + m_prev - m_new), ML_DH)
        c_ref[h] = decay * cst + _dot_tn((w_s * v).astype(BF16), kb)
        n_ref[h:h + 1, :] = decay * nst + jnp.sum(w_s * k, axis=0, keepdims=True)
        m_ref[h:h + 1, :] = m_new

        hn = _rms(hh, norm_ref[:, sl])
        h_ref[:, sl] = (_sigmoid(o_ref[:, sl]) * hn).astype(h_ref.dtype)


def _mlstm_prompt(proj, gates, b_if_pad, norm, c0, n0, m0, n_seq, seq, L):
    nc = seq // L
    blk = lambda off: pl.BlockSpec((L, ML_W), lambda b, c, off=off: (b * nc + c, off // ML_W))
    const2 = lambda b, c: (0, 0)
    return pl.pallas_call(
        functools.partial(_mlstm_chunk_kernel, L=L),
        grid=(n_seq, nc),
        in_specs=[
            blk(R_MQ), blk(R_MK), blk(R_MV), blk(R_MO),
            pl.BlockSpec((L, GATE_W), lambda b, c: (b * nc + c, 0)),
            pl.BlockSpec((1, GATE_W), const2),
            pl.BlockSpec((1, ML_W), const2),
            pl.BlockSpec((None, N_HEADS_ML, ML_DH, ML_DH), lambda b, c: (b, 0, 0, 0)),
            pl.BlockSpec((None, N_HEADS_ML, ML_DH), lambda b, c: (b, 0, 0)),
            pl.BlockSpec((None, N_HEADS_ML, 128), lambda b, c: (b, 0, 0)),
        ],
        out_specs=[
            pl.BlockSpec((L, ML_W), lambda b, c: (b * nc + c, 0)),
            pl.BlockSpec((None, N_HEADS_ML, ML_DH, ML_DH), lambda b, c: (b, 0, 0, 0)),
            pl.BlockSpec((None, N_HEADS_ML, ML_DH), lambda b, c: (b, 0, 0)),
            pl.BlockSpec((None, N_HEADS_ML, 128), lambda b, c: (b, 0, 0)),
        ],
        out_shape=[
            jax.ShapeDtypeStruct((n_seq * seq, ML_W), BF16),
            jax.ShapeDtypeStruct((n_seq, N_HEADS_ML, ML_DH, ML_DH), F32),
            jax.ShapeDtypeStruct((n_seq, N_HEADS_ML, ML_DH), F32),
            jax.ShapeDtypeStruct((n_seq, N_HEADS_ML, 128), F32),
        ],
        compiler_params=_cparams(("parallel", "arbitrary")),
        name="mlstm_prompt",
    )(proj, proj, proj, proj, gates, b_if_pad, norm, c0, n0, m0)


def _mlstm_step_kernel(q_ref, k_ref, v_ref, o_ref, g_ref, bias_ref, norm_ref, c0_ref, n0_ref,
                       m0_ref, h_ref, c_ref, n_ref, m_ref):
    b = pl.program_id(0)
    g = g_ref[pl.ds(b, 1), :] + bias_ref[...]
    r = lax.broadcasted_iota(jnp.int32, (ML_DH, ML_DH), 0)
    c = lax.broadcasted_iota(jnp.int32, (ML_DH, ML_DH), 1)
    eye = r == c
    for h in range(N_HEADS_ML):
        sl = slice(h * ML_DH, (h + 1) * ML_DH)
        ig = g[:, h:h + 1]
        lf = _log_sigmoid(g[:, N_HEADS_ML + h:N_HEADS_ML + h + 1])
        m_prev = m0_ref[h:h + 1, 0:1]
        inter = lf + m_prev
        m_t = jnp.maximum(inter, ig)
        w_i = jnp.exp(ig - m_t)
        w_state = jnp.exp(inter - m_t)

        q = q_ref[pl.ds(b, 1), sl]
        k = k_ref[pl.ds(b, 1), sl] * (ML_DH ** -0.5)
        v = v_ref[pl.ds(b, 1), sl]
        cst = c0_ref[h]
        nst = n0_ref[h:h + 1, :]
        a = w_i * jnp.sum(q * k, axis=1, keepdims=True)
        cq = _dot_nt(jnp.broadcast_to(q, (8, ML_DH)).astype(BF16), cst.astype(BF16))[0:1]
        num = a * v + w_state * cq
        den = a + w_state * jnp.sum(q * nst, axis=1, keepdims=True)
        hh = num / jnp.maximum(jnp.abs(den), jnp.exp(-m_t))

        v_col = jnp.sum(jnp.where(eye, w_i * v, 0.0), axis=1, keepdims=True)
        c_ref[h] = w_state * cst + v_col * k
        n_ref[h:h + 1, :] = w_state * nst + w_i * k
        m_ref[h:h + 1, :] = jnp.broadcast_to(m_t, (1, 128))

        hn = _rms(hh, norm_ref[:, sl])
        h_ref[:, sl] = _sigmoid(o_ref[pl.ds(b, 1), sl]) * hn


def _mlstm_step(proj, gates, b_if_pad, norm, c0, n0, m0):
    bd = proj.shape[0]
    blk = lambda off: pl.BlockSpec((bd, ML_W), lambda b, off=off: (0, off // ML_W))
    const2 = lambda b: (0, 0)
    st4 = pl.BlockSpec((None, N_HEADS_ML, ML_DH, ML_DH), lambda b: (b, 0, 0, 0))
    st3 = pl.BlockSpec((None, N_HEADS_ML, ML_DH), lambda b: (b, 0, 0))
    stm = pl.BlockSpec((None, N_HEADS_ML, 128), lambda b: (b, 0, 0))
    out = pl.pallas_call(
        _mlstm_step_kernel,
        grid=(bd,),
        in_specs=[blk(R_MQ), blk(R_MK), blk(R_MV), blk(R_MO),
                  pl.BlockSpec((bd, GATE_W), const2),
                  pl.BlockSpec((1, GATE_W), const2),
                  pl.BlockSpec((1, ML_W), const2),
                  st4, st3, stm],
        out_specs=[pl.BlockSpec((None, 1, ML_W), lambda b: (b, 0, 0)), st4, st3, stm],
        out_shape=[
            jax.ShapeDtypeStruct((bd, 1, ML_W), F32),
            jax.ShapeDtypeStruct((bd, N_HEADS_ML, ML_DH, ML_DH), F32),
            jax.ShapeDtypeStruct((bd, N_HEADS_ML, ML_DH), F32),
            jax.ShapeDtypeStruct((bd, N_HEADS_ML, 128), F32),
        ],
        compiler_params=_cparams(("parallel",)),
        name="mlstm_step",
    )(proj, proj, proj, proj, gates, b_if_pad, norm, c0, n0, m0)
    return (out[0].reshape(bd, ML_W),) + tuple(out[1:])


def _conv3_rows(u, w_ref, prev8):
    w0, w1, w2 = w_ref[0:1, :], w_ref[1:2, :], w_ref[2:3, :]
    y = w0 * pltpu.roll(u, 2, 0) + w1 * pltpu.roll(u, 1, 0) + w2 * u
    head = u[0:8, :]
    row = lax.broadcasted_iota(jnp.int32, head.shape, 0)
    p1 = prev8[7:8, :]
    p2 = prev8[6:7, :]
    h1 = jnp.where(row == 0, p1, pltpu.roll(head, 1, 0))
    h2 = jnp.where(row == 0, p2, jnp.where(row == 1, p1, pltpu.roll(head, 2, 0)))
    return jnp.concatenate([w0 * h2 + w1 * h1 + w2 * head, y[8:, :]], axis=0)


def _conv3_state(u, w_ref, s0, s1):
    return w_ref[0:1, :] * s0 + w_ref[1:2, :] * s1 + w_ref[2:3, :] * u


def _outproj_prompt_kernel(x_ref, a_ref, hm_ref, cb_ref, cc_ref, cx_ref, cw_ref, w_ref,
                           o_ref, st_ref, carry_ref, *, tiles_per_seq):
    i = pl.program_id(0)
    u = cc_ref[...] * cx_ref[...]
    prev8 = jnp.where(i % tiles_per_seq == 0, 0.0, carry_ref[...])
    yc = cb_ref[...] * _conv3_rows(u, cw_ref, prev8)
    tail = u[u.shape[0] - 8:, :]
    carry_ref[...] = tail
    st_ref[...] = tail
    acc = _dot(a_ref[...], w_ref[0:ATT_W, :])
    acc = acc + _dot(hm_ref[...], w_ref[ATT_W:ATT_W + ML_W, :])
    acc = acc + _dot(yc.astype(BF16), w_ref[ATT_W + ML_W:, :])
    o_ref[...] = x_ref[...] + acc


def _outproj_prompt(x, a, hm, proj, conv_w, w_out, wl, n_seq, seq, tm):
    m = x.shape[0]
    tps = seq // tm
    cblk = lambda off: pl.BlockSpec((tm, CONV_C), lambda i, off=off: (i, off // CONV_C))
    return pl.pallas_call(
        functools.partial(_outproj_prompt_kernel, tiles_per_seq=tps),
        grid=(m // tm,),
        in_specs=[
            pl.BlockSpec((tm, D_MODEL), lambda i: (i, 0)),
            pl.BlockSpec((tm, ATT_W), lambda i: (i, 0)),
            pl.BlockSpec((tm, ML_W), lambda i: (i, 0)),
            cblk(R_CB), cblk(R_CC), cblk(R_CX),
            pl.BlockSpec((3, CONV_C), lambda i: (0, 0)),
            pl.BlockSpec((None, D_MODEL, D_MODEL), lambda i: (wl, 0, 0)),
        ],
        out_specs=[
            pl.BlockSpec((tm, D_MODEL), lambda i: (i, 0)),
            pl.BlockSpec((None, 8, CONV_C), lambda i: (i, 0, 0)),
        ],
        out_shape=[
            jax.ShapeDtypeStruct((m, D_MODEL), F32),
            jax.ShapeDtypeStruct((m // tm, 8, CONV_C), F32),
        ],
        scratch_shapes=[pltpu.VMEM((8, CONV_C), F32)],
        compiler_params=_cparams(("arbitrary",)),
        name="outproj_prompt",
    )(x, a, hm, proj, proj, proj, conv_w, w_out)


def _outproj_step_kernel(x_ref, a_ref, hm_ref, cb_ref, cc_ref, cx_ref, cw_ref, s0_ref, s1_ref,
                         w_ref, o_ref, u_ref):
    u = cc_ref[...] * cx_ref[...]
    u_ref[...] = u
    yc = cb_ref[...] * _conv3_state(u, cw_ref, s0_ref[...], s1_ref[...])
    acc = _dot(a_ref[...].astype(BF16), w_ref[0:ATT_W, :])
    acc = acc + _dot(hm_ref[...].astype(BF16), w_ref[ATT_W:ATT_W + ML_W, :])
    acc = acc + _dot(yc.astype(BF16), w_ref[ATT_W + ML_W:, :])
    o_ref[...] = x_ref[...] + acc


def _outproj_step(x, a, hm, proj, conv_w, s0, s1, w_out, wl):
    bd = x.shape[0]
    full = lambda shp: pl.BlockSpec(shp, lambda i: (0, 0))
    cblk = lambda off: pl.BlockSpec((bd, CONV_C), lambda i, off=off: (0, off // CONV_C))
    return pl.pallas_call(
        _outproj_step_kernel,
        grid=(1,),
        in_specs=[full((bd, D_MODEL)), full((bd, ATT_W)), full((bd, ML_W)),
                  cblk(R_CB), cblk(R_CC), cblk(R_CX),
                  full((3, CONV_C)), full((bd, CONV_C)), full((bd, CONV_C)),
                  pl.BlockSpec((None, D_MODEL, D_MODEL), lambda i: (wl, 0, 0))],
        out_specs=[full((bd, D_MODEL)), full((bd, CONV_C))],
        out_shape=[jax.ShapeDtypeStruct((bd, D_MODEL), F32),
                   jax.ShapeDtypeStruct((bd, CONV_C), F32)],
        compiler_params=_cparams(("arbitrary",)),
        name="outproj_step",
    )(x, a, hm, proj, proj, proj, conv_w, s0, s1, w_out)


def _ffn_prompt_kernel(x_ref, g_ref, wg_ref, wv_ref, cwg_ref, cwv_ref, wd_ref, gfin_ref,
                       o_ref, stg_ref, stv_ref, h_ref, carry_ref, *, tiles_per_seq, final_norm):
    i = pl.program_id(0)
    j = pl.program_id(1)

    @pl.when(j == 0)
    def _():
        x = x_ref[...]
        h_ref[...] = _rms(x, g_ref[...]).astype(BF16)
        o_ref[...] = x

    h = h_ref[...]
    ug = _dot(h, wg_ref[...])
    uv = _dot(h, wv_ref[...])
    start = i % tiles_per_seq == 0
    pg = jnp.where(start, 0.0, carry_ref[j, 0])
    pv = jnp.where(start, 0.0, carry_ref[j, 1])
    cg = _conv3_rows(ug, cwg_ref, pg)
    cv = _conv3_rows(uv, cwv_ref, pv)
    tm = ug.shape[0]
    tg = ug[tm - 8:, :]
    tv = uv[tm - 8:, :]
    carry_ref[j, 0] = tg
    carry_ref[j, 1] = tv
    stg_ref[...] = tg
    stv_ref[...] = tv
    act = (cg * _sigmoid(cg) * cv).astype(BF16)
    o_ref[...] += _dot(act, wd_ref[...])
    if final_norm:
        @pl.when(j == pl.num_programs(1) - 1)
        def _():
            o_ref[...] = _rms(o_ref[...], gfin_ref[...])


def _ffn_prompt(x, g, w_up, ffn_conv_w, w_down, wl, g_final, final_norm, n_seq, seq, tm, tf):
    m = x.shape[0]
    tps = seq // tm
    nj = D_FF // tf
    return pl.pallas_call(
        functools.partial(_ffn_prompt_kernel, tiles_per_seq=tps, final_norm=final_norm),
        grid=(m // tm, nj),
        in_specs=[
            pl.BlockSpec((tm, D_MODEL), lambda i, j: (i, 0)),
            pl.BlockSpec((1, D_MODEL), lambda i, j: (0, 0)),
            pl.BlockSpec((None, D_MODEL, tf), lambda i, j: (wl, 0, j)),
            pl.BlockSpec((None, D_MODEL, tf), lambda i, j: (wl, 0, nj + j)),
            pl.BlockSpec((3, tf), lambda i, j: (0, j)),
            pl.BlockSpec((3, tf), lambda i, j: (0, nj + j)),
            pl.BlockSpec((None, tf, D_MODEL), lambda i, j: (wl, j, 0)),
            pl.BlockSpec((1, D_MODEL), lambda i, j: (0, 0)),
        ],
        out_specs=[
            pl.BlockSpec((tm, D_MODEL), lambda i, j: (i, 0)),
            pl.BlockSpec((None, 8, tf), lambda i, j: (i, 0, j)),
            pl.BlockSpec((None, 8, tf), lambda i, j: (i, 0, j)),
        ],
        out_shape=[
            jax.ShapeDtypeStruct((m, D_MODEL), F32),
            jax.ShapeDtypeStruct((m // tm, 8, D_FF), F32),
            jax.ShapeDtypeStruct((m // tm, 8, D_FF), F32),
        ],
        scratch_shapes=[pltpu.VMEM((tm, D_MODEL), BF16), pltpu.VMEM((nj, 2, 8, tf), F32)],
        compiler_params=_cparams(("arbitrary", "arbitrary")),
        name="ffn_prompt",
    )(x, g, w_up, w_up, ffn_conv_w, ffn_conv_w, w_down, g_final)


def _ffn_step_kernel(x_ref, g_ref, wg_ref, wv_ref, cwg_ref, cwv_ref, s0g_ref, s1g_ref, s0v_ref,
                     s1v_ref, wd_ref, gfin_ref, o_ref, ug_ref, uv_ref, h_ref, *, final_norm):
    @pl.when(pl.program_id(0) == 0)
    def _():
        x = x_ref[...]
        h_ref[...] = _rms(x, g_ref[...]).astype(BF16)
        o_ref[...] = x

    h = h_ref[...]
    ug = _dot(h, wg_ref[...])
    uv = _dot(h, wv_ref[...])
    ug_ref[...] = ug
    uv_ref[...] = uv
    cg = _conv3_state(ug, cwg_ref, s0g_ref[...], s1g_ref[...])
    cv = _conv3_state(uv, cwv_ref, s0v_ref[...], s1v_ref[...])
    act = (cg * _sigmoid(cg) * cv).astype(BF16)
    o_ref[...] += _dot(act, wd_ref[...])
    if final_norm:
        @pl.when(pl.program_id(0) == pl.num_programs(0) - 1)
        def _():
            o_ref[...] = _rms(o_ref[...], gfin_ref[...])


def _ffn_step(x, g, w_up, ffn_conv_w, s0, s1, w_down, wl, g_final, final_norm, tf):
    bd = x.shape[0]
    nj = D_FF // tf
    gcol = lambda rows: pl.BlockSpec((rows, tf), lambda j: (0, j))
    vcol = lambda rows: pl.BlockSpec((rows, tf), lambda j: (0, nj + j))
    return pl.pallas_call(
        functools.partial(_ffn_step_kernel, final_norm=final_norm),
        grid=(nj,),
        in_specs=[
            pl.BlockSpec((bd, D_MODEL), lambda j: (0, 0)),
            pl.BlockSpec((1, D_MODEL), lambda j: (0, 0)),
            pl.BlockSpec((None, D_MODEL, tf), lambda j: (wl, 0, j)),
            pl.BlockSpec((None, D_MODEL, tf), lambda j: (wl, 0, nj + j)),
            gcol(3), vcol(3),
            gcol(bd), gcol(bd), vcol(bd), vcol(bd),
            pl.BlockSpec((None, tf, D_MODEL), lambda j: (wl, j, 0)),
            pl.BlockSpec((1, D_MODEL), lambda j: (0, 0)),
        ],
        out_specs=[pl.BlockSpec((bd, D_MODEL), lambda j: (0, 0)), gcol(bd), gcol(bd)],
        out_shape=[jax.ShapeDtypeStruct((bd, D_MODEL), F32),
                   jax.ShapeDtypeStruct((bd, D_FF), F32),
                   jax.ShapeDtypeStruct((bd, D_FF), F32)],
        scratch_shapes=[pltpu.VMEM((bd, D_MODEL), BF16)],
        compiler_params=_cparams(("arbitrary",)),
        name="ffn_step",
    )(x, g, w_up, w_up, ffn_conv_w, ffn_conv_w, s0, s1, s0, s1, w_down, g_final)


def _tiles(seq):
    tm = min(512, seq)
    tmi = min(1024, seq)
    tf = 512
    tq = min(256, seq)
    rs = 2 * tq
    lc = min(512, seq)
    return tm, tmi, tf, tq, rs, lc


def kernel(x_prompt, x_sample, cache_k, cache_v, state_mlstm_c, state_mlstm_n, state_mlstm_m,
           state_conv, state_ffn_conv, page_table,
           g_mix, w_in, b_if, att_lambda, att_subln, mlstm_norm, conv_w, w_out,
           g_ffn, w_up, ffn_conv_w, w_down, g_final):
    depth = w_in.shape[0]
    n_seq, seq, _ = x_prompt.shape
    bd = x_sample.shape[0]
    n_pool = cache_k.shape[1]
    tm, tmi, tf, tq, rs, lc = _tiles(seq)
    assert seq % tm == 0 and seq % tmi == 0 and seq % lc == 0 and tm % 8 == 0
    assert seq % (2 * tq) == 0 and tq % ATT_HW == 0, "attention pairs query tiles of whole lane tiles"
    assert D_FF % tf == 0 and page_table.shape[1] >= 1
    tps = seq // tm
    n_pages = page_table.shape[1]
    pages_step = math.gcd(32, n_pages)

    n_gate = 2 * N_HEADS_ML
    w_bf = w_in.transpose(0, 2, 1).astype(BF16)
    w_conv = w_bf[:, N_QKV_MLSTM + n_gate:, :]
    w_out_b = w_out.astype(BF16)
    w_up_b = w_up.astype(BF16)
    w_down_b = w_down.astype(BF16)
    b_if_pad = jnp.pad(b_if, ((0, 0), (0, GATE_W - n_gate))).reshape(depth, 1, GATE_W)

    slopes = jnp.asarray(_alibi_slopes(N_HEADS_ATT))
    cache_kt = cache_k.transpose(0, 1, 3, 2, 4)
    cache_vt = cache_v.transpose(0, 1, 3, 2, 4)

    c0p = jnp.zeros((n_seq, N_HEADS_ML, ML_DH, ML_DH), F32)
    n0p = jnp.zeros((n_seq, N_HEADS_ML, ML_DH), F32)
    m0p = jnp.full((n_seq, N_HEADS_ML, 128), -jnp.inf, F32)

    xp = x_prompt.reshape(n_seq * seq, D_MODEL)
    xs = x_sample.reshape(bd, D_MODEL)
    outs_p = [[] for _ in range(7)]
    outs_s = [[] for _ in range(7)]
    k_all = v_all = None
    gfin = g_final.reshape(1, D_MODEL)
    for l in range(depth):
        last = l == depth - 1
        lam_init = 0.8 - 0.6 * math.exp(-0.3 * l)
        gm = g_mix[l].reshape(1, D_MODEL)
        gf = g_ffn[l].reshape(1, D_MODEL)
        sub = att_subln[l].reshape(1, ATT_HW)
        mnorm = mlstm_norm[l].reshape(1, ML_W)

        proj, gates, q, k_all, v_all = _inproj(xp, gm, w_bf, w_conv, l, n_seq, seq, tmi, l, depth,
                                               k_all, v_all)
        a = _attn_prompt(q, k_all, v_all, l, slopes, att_lambda[l], sub, n_seq, seq, tq, rs, lam_init)
        hm, c1, n1, m1 = _mlstm_prompt(proj, gates, b_if_pad[l], mnorm, c0p, n0p, m0p, n_seq, seq, lc)
        xp, cst = _outproj_prompt(xp, a, hm, proj, conv_w[l], w_out_b, l, n_seq, seq, tm)
        xp, stg, stv = _ffn_prompt(xp, gf, w_up_b, ffn_conv_w[l], w_down_b, l, gfin, last,
                                   n_seq, seq, tm, tf)
        outs_p[2].append(c1)
        outs_p[3].append(n1)
        outs_p[4].append(m1[:, :, 0])
        outs_p[5].append(cst[tps - 1::tps, 6:8, :])
        outs_p[6].append(jnp.concatenate([stg[tps - 1::tps, 6:8, :], stv[tps - 1::tps, 6:8, :]], axis=-1))

        proj_s, gates_s, q_s, k_s, v_s = _inproj(xs, gm, w_bf, w_conv, l, 1, bd, bd, 0, 1)
        q_s, k_s, v_s = q_s[0], k_s[0, 0], v_s[0, 0]
        scores = _attn_scores(q_s, cache_kt, page_table, l, pages_step)
        a_s = _attn_pv(scores, q_s, k_s, v_s, att_lambda[l], sub, cache_vt, page_table, l,
                       pages_step, lam_init)
        m0s = jnp.broadcast_to(state_mlstm_m[l][:, :, None], (bd, N_HEADS_ML, 128))
        hm_s, c1s, n1s, m1s = _mlstm_step(proj_s, gates_s, b_if_pad[l], mnorm, state_mlstm_c[l],
                                          state_mlstm_n[l], m0s)
        sc = state_conv[l]
        xs, u_s = _outproj_step(xs, a_s, hm_s, proj_s, conv_w[l], sc[:, 0], sc[:, 1], w_out_b, l)
        sf = state_ffn_conv[l]
        xs, ug_s, uv_s = _ffn_step(xs, gf, w_up_b, ffn_conv_w[l], sf[:, 0], sf[:, 1], w_down_b, l,
                                   gfin, last, tf)
        outs_s[0].append(k_s.transpose(1, 0, 2).reshape(bd, 1, N_HEADS_ATT, ATT_HW))
        outs_s[1].append(v_s.transpose(1, 0, 2).reshape(bd, 1, N_HEADS_ATT, ATT_HW))
        outs_s[2].append(c1s)
        outs_s[3].append(n1s)
        outs_s[4].append(m1s[:, :, 0])
        outs_s[5].append(jnp.stack([sc[:, 1], u_s], axis=1))
        outs_s[6].append(jnp.stack([sf[:, 1], jnp.concatenate([ug_s, uv_s], axis=-1)], axis=1))

    y_prompt = xp.reshape(n_seq, seq, D_MODEL)
    y_sample = xs.reshape(bd, 1, D_MODEL)
    c_p, n_p, m_p, conv_p, ffn_p = [jnp.stack(o, 0) for o in outs_p[2:]]
    k_p = k_all.transpose(0, 1, 3, 2, 4)
    v_p = v_all.transpose(0, 1, 3, 2, 4)
    k_s_, v_s_, c_s, n_s, m_s, conv_s, ffn_s = [jnp.stack(o, 0) for o in outs_s]
    return (y_prompt, y_sample, k_p, v_p, k_s_, v_s_, c_p, n_p, m_p, c_s, n_s, m_s,
            conv_p, conv_s, ffn_p, ffn_s)
```
